```python
import jax, jax.numpy as jnp
from jax import lax
import numpy as np

D_MODEL = 2048
BATCH = 16
SEQ = 2048
DEPTH = 1
DEC_BATCH = 32
DEC_SEQ = 4
PAST_LEN = 16384
PAGE_SIZE = 128

H_RET = 4
DK_RET = 256
DV_RET = 256
W_RET = H_RET * DV_RET
CHUNK = 128
ROPE_BASE = 10000.0
H_FOX = 8
HD_FOX = 128
W_FOX = H_FOX * HD_FOX
Q_BLOCK = 128
FGT_BIAS_MEAN = 4.0
MIX_WIDTH = W_RET + W_FOX
SPLIT_SIZES = (H_RET * DK_RET, H_RET * DK_RET, W_RET, W_RET, W_FOX, W_FOX, W_FOX, W_FOX, H_FOX, D_MODEL, D_MODEL)
N_COLS = 2 * H_RET * DK_RET + 2 * W_RET + 4 * W_FOX + H_FOX + 2 * D_MODEL
EPS = 1e-6

kernel_name = 'retention_fox_gated_hybrid_step'


def _rmsnorm(x, g):
    xf = x.astype(jnp.float32)
    y = xf * lax.rsqrt(jnp.mean(xf * xf, axis=-1, keepdims=True) + EPS)
    return (y * g.astype(jnp.float32)).astype(x.dtype)


def _rope(x, pos):
    d = x.shape[-1]
    inv = 1.0 / (ROPE_BASE ** (jnp.arange(0, d, 2, dtype=jnp.float32) / d))
    ang = pos.astype(jnp.float32)[:, None] * inv[None, :]
    cos = jnp.cos(ang)[None, :, None, :]
    sin = jnp.sin(ang)[None, :, None, :]
    x1, x2 = jnp.split(x.astype(jnp.float32), 2, axis=-1)
    return jnp.concatenate([x1 * cos - x2 * sin, x1 * sin + x2 * cos], axis=-1).astype(x.dtype)


def _log_gamma():
    return jnp.log(1.0 - 2.0 ** (-5.0 - jnp.arange(H_RET, dtype=jnp.float32)))


def _branch_inputs(x, c, pos, w_in, b_fgt, g_norm, w_ada, b_ada):
    B, L, _ = x.shape
    mod = jax.nn.silu(c) @ w_ada + b_ada
    shift, scale, gate = jnp.split(mod[:, None, :], 3, axis=-1)
    h = _rmsnorm(x, g_norm) * (1 + scale) + shift
    z = h @ w_in
    bounds = np.cumsum(SPLIT_SIZES)[:-1].tolist()
    q_r, k_r, v_r, g_r, q_f, k_f, v_f, g_f, f_logit, m_r, m_f = jnp.split(z, bounds, axis=-1)
    q_r = _rope(q_r.reshape(B, L, H_RET, DK_RET), pos)
    k_r = _rope(k_r.reshape(B, L, H_RET, DK_RET), pos) * (DK_RET ** -0.5)
    v_r = v_r.reshape(B, L, H_RET, DV_RET)
    q_f = q_f.reshape(B, L, H_FOX, HD_FOX)
    k_f = k_f.reshape(B, L, H_FOX, HD_FOX)
    v_f = v_f.reshape(B, L, H_FOX, HD_FOX)
    logf = jax.nn.log_sigmoid((f_logit + b_fgt).astype(jnp.float32))
    return q_r, k_r, v_r, g_r, q_f, k_f, v_f, g_f, logf, m_r, m_f, gate


def _retention_chunk(q, k, v, state):
    L = q.shape[1]
    lg = _log_gamma()
    idx = jnp.arange(L, dtype=jnp.float32)
    diff = idx[:, None] - idx[None, :]
    causal = diff >= 0
    dmask = jnp.where(causal[None], jnp.exp(lg[:, None, None] * jnp.where(causal, diff, 0.0)[None]), 0.0)
    qf = q.astype(jnp.float32)
    kf = k.astype(jnp.float32)
    vf = v.astype(jnp.float32)
    scores = jnp.einsum('bihd,bjhd->bhij', qf, kf) * dmask[None]
    o = jnp.einsum('bhij,bjhe->bihe', scores, vf)
    q_dec = qf * jnp.exp(lg[None, :] * (idx[:, None] + 1.0))[None, :, :, None]
    o = o + jnp.einsum('bihd,bhde->bihe', q_dec, state)
    k_dec = kf * jnp.exp(lg[None, :] * (L - 1.0 - idx[:, None]))[None, :, :, None]
    new_state = jnp.exp(lg * L)[None, :, None, None] * state + jnp.einsum('bjhd,bjhe->bhde', k_dec, vf)
    return o, new_state


def _retention_prompt(q, k, v):
    B, S, H, _ = q.shape
    nc = S // CHUNK

    def to_chunks(a):
        return a.reshape(B, nc, CHUNK, *a.shape[2:]).swapaxes(0, 1)

    def step(st, qkv):
        o, st_new = _retention_chunk(qkv[0], qkv[1], qkv[2], st)
        return st_new, o

    s0 = jnp.zeros((B, H, DK_RET, DV_RET), jnp.float32)
    s_fin, o = lax.scan(step, s0, (to_chunks(q), to_chunks(k), to_chunks(v)))
    return o.swapaxes(0, 1).reshape(B, S, H, DV_RET), s_fin


def _fox_prompt(q, k, v, logf):
    B, S, H, D = q.shape
    scale = D ** -0.5
    F = jnp.cumsum(logf, axis=1).transpose(0, 2, 1)
    qh = q.transpose(0, 2, 1, 3)
    kh = k.transpose(0, 2, 1, 3)
    vh = v.transpose(0, 2, 1, 3)
    kpos = jnp.arange(S)

    def block(i):
        start = i * Q_BLOCK
        qb = lax.dynamic_slice_in_dim(qh, start, Q_BLOCK, axis=2)
        Fb = lax.dynamic_slice_in_dim(F, start, Q_BLOCK, axis=2)
        s = jnp.einsum('bhqd,bhkd->bhqk', qb, kh).astype(jnp.float32) * scale + Fb[..., :, None] - F[..., None, :]
        qpos = start + jnp.arange(Q_BLOCK)
        s = jnp.where(qpos[:, None] >= kpos[None, :], s, -jnp.inf)
        p = jax.nn.softmax(s, axis=-1)
        return jnp.einsum('bhqk,bhkd->bhqd', p.astype(v.dtype), vh)

    o = lax.map(block, jnp.arange(S // Q_BLOCK))
    return o.transpose(1, 0, 3, 2, 4).reshape(B, S, H, D)


def _fox_sample(q, k, v, logf, cache_k, cache_v, cache_logf, page_table):
    B, T, H, D = q.shape
    scale = D ** -0.5
    k_past = cache_k[page_table].reshape(B, -1, H, D)
    v_past = cache_v[page_table].reshape(B, -1, H, D)
    lf_past = cache_logf[page_table].reshape(B, -1, H).astype(jnp.float32)
    P = k_past.shape[1]
    R = lax.cumsum(lf_past, axis=1, reverse=True) - lf_past
    G = jnp.cumsum(logf, axis=1).transpose(0, 2, 1)
    s_past = (jnp.einsum('bthd,bshd->bhts', q, k_past.astype(q.dtype)).astype(jnp.float32) * scale
              + G[..., :, None] + R.transpose(0, 2, 1)[:, :, None, :])
    s_new = jnp.einsum('bthd,bshd->bhts', q, k).astype(jnp.float32) * scale + G[..., :, None] - G[..., None, :]
    tpos = jnp.arange(T)
    s_new = jnp.where(tpos[:, None] >= tpos[None, :], s_new, -jnp.inf)
    p = jax.nn.softmax(jnp.concatenate([s_past, s_new], axis=-1), axis=-1).astype(v.dtype)
    return (jnp.einsum('bhts,bshd->bthd', p[..., :P], v_past.astype(v.dtype))
            + jnp.einsum('bhts,bshd->bthd', p[..., P:], v))


def _merge(x, o_r, g_r, o_f, g_f, m_r, m_f, gate, w_branch, w_out):
    B, L, _ = x.shape
    o_r = o_r * lax.rsqrt(jnp.mean(o_r * o_r, axis=-1, keepdims=True) + EPS)
    a_r = o_r.reshape(B, L, W_RET).astype(x.dtype) * jax.nn.silu(g_r)
    a_f = o_f.reshape(B, L, W_FOX).astype(x.dtype) * jax.nn.silu(g_f)
    p_r = a_r @ w_branch[:W_RET]
    p_f = a_f @ w_branch[W_RET:]
    merged = jax.nn.sigmoid(m_r) * p_r + jax.nn.sigmoid(m_f) * p_f
    return x + gate * (merged @ w_out)


def setup_inputs(seed: int = 0) -> dict:
    key = jax.random.key(seed)
    ks = jax.random.split(key, 20)
    n_pages = PAST_LEN // PAGE_SIZE
    n_pool = (5 * DEC_BATCH * n_pages) // 4
    f32 = jnp.float32
    x_prompt = jax.random.normal(ks[0], (BATCH, SEQ, D_MODEL), f32)
    x_sample = jax.random.normal(ks[1], (DEC_BATCH, DEC_SEQ, D_MODEL), f32)
    c_prompt = jax.random.normal(ks[2], (BATCH, D_MODEL), f32)
    c_sample = jax.random.normal(ks[3], (DEC_BATCH, D_MODEL), f32)
    cache_k = jax.random.normal(ks[4], (DEPTH, n_pool, PAGE_SIZE, H_FOX, HD_FOX), f32)
    cache_v = jax.random.normal(ks[5], (DEPTH, n_pool, PAGE_SIZE, H_FOX, HD_FOX), f32)
    cache_logf = jax.nn.log_sigmoid(FGT_BIAS_MEAN + jax.random.normal(ks[6], (DEPTH, n_pool, PAGE_SIZE, H_FOX), f32))
    state_ret = 0.1 * jax.random.normal(ks[7], (DEPTH, DEC_BATCH, H_RET, DK_RET, DV_RET), f32)
    page_table = jax.random.permutation(ks[8], n_pool)[:DEC_BATCH * n_pages].reshape(DEC_BATCH, n_pages).astype(jnp.int32)
    w_in = jax.random.normal(ks[9], (DEPTH, D_MODEL, N_COLS), f32) * D_MODEL ** -0.5
    b_fgt = FGT_BIAS_MEAN + 0.5 * jax.random.normal(ks[10], (DEPTH, H_FOX), f32)
    g_norm = 1.0 + 0.01 * jax.random.normal(ks[11], (DEPTH, D_MODEL), f32)
    w_ada = 0.5 * jax.random.normal(ks[12], (DEPTH, D_MODEL, 3 * D_MODEL), f32) * D_MODEL ** -0.5
    b_ada = 0.01 * jax.random.normal(ks[13], (DEPTH, 3 * D_MODEL), f32)
    w_branch = jax.random.normal(ks[14], (DEPTH, MIX_WIDTH, D_MODEL), f32) * MIX_WIDTH ** -0.5
    w_out = jax.random.normal(ks[15], (DEPTH, D_MODEL, D_MODEL), f32) * D_MODEL ** -0.5
    g_final = 1.0 + 0.01 * jax.random.normal(ks[16], (D_MODEL,), f32)
    return {'x_prompt': x_prompt, 'x_sample': x_sample, 'c_prompt': c_prompt, 'c_sample': c_sample,
            'cache_k': cache_k, 'cache_v': cache_v, 'cache_logf': cache_logf, 'state_ret': state_ret,
            'page_table': page_table, 'w_in': w_in, 'b_fgt': b_fgt, 'g_norm': g_norm, 'w_ada': w_ada,
            'b_ada': b_ada, 'w_branch': w_branch, 'w_out': w_out, 'g_final': g_final}


def reference(x_prompt, x_sample, c_prompt, c_sample, cache_k, cache_v, cache_logf, state_ret,
              page_table, w_in, b_fgt, g_norm, w_ada, b_ada, w_branch, w_out, g_final):
    past_len = page_table.shape[1] * PAGE_SIZE
    pos_p = jnp.arange(x_prompt.shape[1])
    pos_s = past_len + jnp.arange(x_sample.shape[1])
    xp, xs = x_prompt, x_sample
    kp_l, vp_l, lp_l, rp_l = [], [], [], []
    ks_l, vs_l, ls_l, rs_l = [], [], [], []
    for l in range(DEPTH):
        q_r, k_r, v_r, g_r, q_f, k_f, v_f, g_f, logf, m_r, m_f, gate = _branch_inputs(
            xp, c_prompt, pos_p, w_in[l], b_fgt[l], g_norm[l], w_ada[l], b_ada[l])
        o_r, st_p = _retention_prompt(q_r, k_r, v_r)
        o_f = _fox_prompt(q_f, k_f, v_f, logf)
        xp = _merge(xp, o_r, g_r, o_f, g_f, m_r, m_f, gate, w_branch[l], w_out[l])
        kp_l.append(k_f)
        vp_l.append(v_f)
        lp_l.append(logf)
        rp_l.append(st_p)
        q_r, k_r, v_r, g_r, q_f, k_f, v_f, g_f, logf, m_r, m_f, gate = _branch_inputs(
            xs, c_sample, pos_s, w_in[l], b_fgt[l], g_norm[l], w_ada[l], b_ada[l])
        o_r, st_s = _retention_chunk(q_r, k_r, v_r, state_ret[l].astype(jnp.float32))
        o_f = _fox_sample(q_f, k_f, v_f, logf, cache_k[l], cache_v[l], cache_logf[l], page_table)
        xs = _merge(xs, o_r, g_r, o_f, g_f, m_r, m_f, gate, w_branch[l], w_out[l])
        ks_l.append(k_f)
        vs_l.append(v_f)
        ls_l.append(logf)
        rs_l.append(st_s)
    y_prompt = _rmsnorm(xp, g_final)
    y_sample = _rmsnorm(xs, g_final)
    return (y_prompt, y_sample, jnp.stack(kp_l), jnp.stack(vp_l), jnp.stack(lp_l), jnp.stack(rp_l),
            jnp.stack(ks_l), jnp.stack(vs_l), jnp.stack(ls_l), jnp.stack(rs_l))
```

```python
import functools

import jax
import jax.numpy as jnp
import numpy as np
from jax import lax
from jax.experimental import pallas as pl
from jax.experimental.pallas import tpu as pltpu

F32 = jnp.float32
BF16 = jnp.bfloat16

D_MODEL = 2048
H_RET, DK_RET, DV_RET = 4, 256, 256
H_FOX, HD_FOX = 8, 128
W_RET = H_RET * DV_RET
W_FOX = H_FOX * HD_FOX
RET_CHUNK = 128
PAGE = 128
ROPE_BASE = 10000.0
EPS = 1e-6
T_PAD = 16
PAGES_PER_STEP = 8
Z_COLS = 12288
F_COL0 = 8192

ZB_QR, ZB_KR, ZB_VR, ZB_GR, ZB_QF, ZB_KF, ZB_VF, ZB_GF = range(8)
VMEM_LIMIT = 56 * 1024 * 1024

NT_DIMS = (((1,), (1,)), ((), ()))
TN_DIMS = (((0,), (0,)), ((), ()))


def _params(sem):
    return pltpu.CompilerParams(dimension_semantics=sem, vmem_limit_bytes=VMEM_LIMIT)


def _split3(x):
    hi = x.astype(BF16)
    r1 = x - hi.astype(F32)
    mid = r1.astype(BF16)
    lo = (r1 - mid.astype(F32)).astype(BF16)
    return hi, mid, lo


def _mod_kernel(c_ref, w_ref, b_ref, o_ref):
    c = c_ref[...]
    sc = c * jax.nn.sigmoid(c)
    o_ref[...] = jnp.dot(sc, w_ref[...], preferred_element_type=F32,
                         precision=lax.Precision.HIGHEST) + b_ref[...]


def _mod(c, w_ada, b_ada, tn=512):
    n, d = c.shape
    cols = w_ada.shape[1]
    return pl.pallas_call(
        _mod_kernel,
        grid=(cols // tn,),
        in_specs=[pl.BlockSpec((n, d), lambda j: (0, 0)),
                  pl.BlockSpec((d, tn), lambda j: (0, j)),
                  pl.BlockSpec((1, tn), lambda j: (0, j))],
        out_specs=pl.BlockSpec((n, tn), lambda j: (0, j)),
        out_shape=jax.ShapeDtypeStruct((n, cols), F32),
        compiler_params=_params(("arbitrary",)),
        name="mod",
    )(c, w_ada, b_ada.reshape(1, cols))


def _proj_kernel(x_ref, scale_ref, shift_ref, gn_ref, w_ref, wf_ref, bf_ref, cos_ref, sin_ref,
                 z_ref, kf_ref, vf_ref, lf_ref, h_scr, *, tn):
    j = pl.program_id(1)
    per_kb = 1024 // tn

    @pl.when(j == 0)
    def _():
        x = x_ref[...]
        ms = jnp.mean(x * x, axis=-1, keepdims=True)
        xn = x * lax.rsqrt(ms + EPS) * gn_ref[...]
        h = xn * (1.0 + scale_ref[0]) + shift_ref[0]
        hb = h.astype(BF16)
        h_scr[...] = hb
        f = jnp.dot(hb, wf_ref[...], preferred_element_type=F32)
        lf_ref[...] = jax.nn.log_sigmoid(f[:, :H_FOX] + bf_ref[...])

    acc = jnp.dot(h_scr[...], w_ref[...], preferred_element_type=F32)
    sec = j // per_kb

    @pl.when(sec <= ZB_KR)
    def _():
        c = cos_ref[...]
        s = sin_ref[...]
        sc = jnp.where(sec == ZB_KR, DK_RET ** -0.5, 1.0).astype(F32)
        for hh in range(tn // DK_RET):
            lo, mid, hi = hh * DK_RET, hh * DK_RET + DK_RET // 2, (hh + 1) * DK_RET
            x1 = acc[:, lo:mid]
            x2 = acc[:, mid:hi]
            z_ref[:, lo:mid] = ((x1 * c - x2 * s) * sc).astype(BF16)
            z_ref[:, mid:hi] = ((x1 * s + x2 * c) * sc).astype(BF16)

    is_silu = (sec == ZB_GR) | (sec == ZB_GF)
    is_sig = sec > ZB_GF

    @pl.when(is_silu)
    def _():
        z_ref[...] = (acc * jax.nn.sigmoid(acc)).astype(BF16)

    @pl.when(is_sig)
    def _():
        z_ref[...] = jax.nn.sigmoid(acc).astype(BF16)

    @pl.when((sec > ZB_KR) & jnp.logical_not(is_silu) & jnp.logical_not(is_sig))
    def _():
        z_ref[...] = acc.astype(BF16)

    @pl.when(sec == ZB_KF)
    def _():
        kf_ref[...] = acc

    @pl.when(sec == ZB_VF)
    def _():
        vf_ref[...] = acc


def _proj(x2d, scale, shift, g_norm, w_main, w_f, b_f, cos, sin, *, tm, tn=512):
    rows, d = x2d.shape
    nrb = rows // tm
    ncb = Z_COLS // tn
    per_kb = 1024 // tn
    groups = scale.shape[0]
    bpg = nrb // groups
    r = scale.shape[1]
    ntab = cos.shape[0] // tm
    kern = functools.partial(_proj_kernel, tn=tn)
    kf0, vf0 = ZB_KF * per_kb, ZB_VF * per_kb
    return pl.pallas_call(
        kern,
        grid=(nrb, ncb),
        in_specs=[
            pl.BlockSpec((tm, d), lambda i, j: (i, 0)),
            pl.BlockSpec((1, r, d), lambda i, j: (i // bpg, 0, 0)),
            pl.BlockSpec((1, r, d), lambda i, j: (i // bpg, 0, 0)),
            pl.BlockSpec((1, d), lambda i, j: (0, 0)),
            pl.BlockSpec((d, tn), lambda i, j: (0, j)),
            pl.BlockSpec((d, 128), lambda i, j: (0, 0)),
            pl.BlockSpec((1, H_FOX), lambda i, j: (0, 0)),
            pl.BlockSpec((tm, 128), lambda i, j: (i % ntab, 0)),
            pl.BlockSpec((tm, 128), lambda i, j: (i % ntab, 0)),
        ],
        out_specs=[
            pl.BlockSpec((tm, tn), lambda i, j: (i, j)),
            pl.BlockSpec((tm, tn), lambda i, j: (i, jnp.clip(j - kf0, 0, per_kb - 1))),
            pl.BlockSpec((tm, tn), lambda i, j: (i, jnp.clip(j - vf0, 0, per_kb - 1))),
            pl.BlockSpec((tm, H_FOX), lambda i, j: (i, 0)),
        ],
        out_shape=[
            jax.ShapeDtypeStruct((rows, Z_COLS), BF16),
            jax.ShapeDtypeStruct((rows, W_FOX), F32),
            jax.ShapeDtypeStruct((rows, W_FOX), F32),
            jax.ShapeDtypeStruct((rows, H_FOX), F32),
        ],
        scratch_shapes=[pltpu.VMEM((tm, d), BF16)],
        compiler_params=_params(("arbitrary", "arbitrary")),
        name="proj",
    )(x2d, scale, shift, g_norm.reshape(1, d), w_main, w_f, b_f.reshape(1, H_FOX), cos, sin)


def _cumsum_kernel(lt_ref, ft_ref, *, chunk):
    s = lt_ref.shape[2]
    row = lax.broadcasted_iota(jnp.int32, (chunk, chunk), 0)
    col = lax.broadcasted_iota(jnp.int32, (chunk, chunk), 1)
    tri = jnp.where(row <= col, 1.0, 0.0).astype(BF16)
    carry = jnp.zeros((H_FOX, 1), F32)
    for c in range(s // chunk):
        lf = lt_ref[0, :, c * chunk:(c + 1) * chunk]
        terms = [t.astype(F32) for t in _split3(lf)] + [jnp.zeros_like(lf)]
        parts = jnp.dot(jnp.concatenate(terms, axis=0).astype(BF16), tri, preferred_element_type=F32)
        cs = parts[0:H_FOX] + parts[H_FOX:2 * H_FOX] + parts[2 * H_FOX:3 * H_FOX] + carry
        ft_ref[0, :, c * chunk:(c + 1) * chunk] = cs
        carry = cs[:, chunk - 1:chunk]


def _cumsum_t(lf_t, chunk=256):
    b, h, s = lf_t.shape
    return pl.pallas_call(
        functools.partial(_cumsum_kernel, chunk=chunk),
        grid=(b,),
        in_specs=[pl.BlockSpec((1, h, s), lambda i: (i, 0, 0))],
        out_specs=pl.BlockSpec((1, h, s), lambda i: (i, 0, 0)),
        out_shape=jax.ShapeDtypeStruct((b, h, s), F32),
        compiler_params=_params(("arbitrary",)),
        name="cumsum",
    )(lf_t)


def _fox_kernel(q_ref, k_ref, v_ref, sg_ref, f_ref, ft_ref, o_ref, *, t, scale):
    h = pl.program_id(1)
    qi = pl.program_id(2)
    q = q_ref[...]
    lane = lax.broadcasted_iota(jnp.int32, f_ref.shape, 1)
    fq = jnp.sum(jnp.where(lane == h, f_ref[...], 0.0), axis=1, keepdims=True)

    def block(kj, carry, masked):
        m, l, acc = carry
        rows = pl.ds(pl.multiple_of(kj * t, t), t)
        k = k_ref[rows, :]
        v = v_ref[rows, :]
        fk = ft_ref[0, kj]
        s = lax.dot_general(q, k, NT_DIMS, preferred_element_type=F32) * scale + fq - fk
        if masked:
            r = lax.broadcasted_iota(jnp.int32, (t, t), 0)
            c = lax.broadcasted_iota(jnp.int32, (t, t), 1)
            s = jnp.where(r >= c, s, -jnp.inf)
        m_new = jnp.maximum(m, jnp.max(s, axis=1, keepdims=True))
        alpha = jnp.exp(m - m_new)
        p = jnp.exp(s - m_new)
        l = alpha * l + jnp.sum(p, axis=1, keepdims=True)
        acc = alpha * acc + jnp.dot(p.astype(BF16), v, preferred_element_type=F32)
        return m_new, l, acc

    init = (jnp.full((t, 1), -jnp.inf, F32), jnp.zeros((t, 1), F32), jnp.zeros((t, HD_FOX), F32))
    carry = lax.fori_loop(0, qi, lambda kj, c: block(kj, c, False), init)
    m, l, acc = block(qi, carry, True)
    o_ref[...] = ((acc / l) * sg_ref[...].astype(F32)).astype(BF16)


def _fox_prompt(z, f_rows, f_t, *, batch, seq, t=512):
    nq = seq // t
    cpb = 1024 // HD_FOX
    kern = functools.partial(_fox_kernel, t=t, scale=HD_FOX ** -0.5)
    return pl.pallas_call(
        kern,
        grid=(batch, H_FOX, nq),
        in_specs=[
            pl.BlockSpec((t, HD_FOX), lambda b, h, i: (b * nq + i, ZB_QF * cpb + h)),
            pl.BlockSpec((seq, HD_FOX), lambda b, h, i: (b, ZB_KF * cpb + h)),
            pl.BlockSpec((seq, HD_FOX), lambda b, h, i: (b, ZB_VF * cpb + h)),
            pl.BlockSpec((t, HD_FOX), lambda b, h, i: (b * nq + i, ZB_GF * cpb + h)),
            pl.BlockSpec((t, H_FOX), lambda b, h, i: (b * nq + i, 0)),
            pl.BlockSpec((1, nq, 1, t), lambda b, h, i: (b * H_FOX + h, 0, 0, 0)),
        ],
        out_specs=pl.BlockSpec((t, HD_FOX), lambda b, h, i: (b * nq + i, h)),
        out_shape=jax.ShapeDtypeStruct((batch * seq, W_FOX), BF16),
        compiler_params=_params(("arbitrary", "arbitrary", "arbitrary")),
        name="fox_prompt",
    )(z, z, z, z, f_rows, f_t)


def _head_norm_gate(o, sg):
    on = o * lax.rsqrt(jnp.mean(o * o, axis=-1, keepdims=True) + EPS)
    return (on * sg.astype(F32)).astype(BF16)


def _ret_kernel(q_ref, k_ref, v_ref, sg_ref, dm_ref, qd_ref, kd_ref, gl_ref, a_ref, st_ref, s_scr, *, chunk):
    s_scr[...] = jnp.zeros_like(s_scr)
    dm = dm_ref[0]
    qd = qd_ref[0]
    kd = kd_ref[0]
    gl = gl_ref[0]
    nc = q_ref.shape[0] // chunk

    def body(c, carry):
        rows = pl.ds(pl.multiple_of(c * chunk, chunk), chunk)
        q = q_ref[rows, :]
        k = k_ref[rows, :]
        v = v_ref[rows, :]
        st = s_scr[...]
        sc = lax.dot_general(q, k, NT_DIMS, preferred_element_type=F32) * dm
        o = (jnp.dot(sc.astype(BF16), v, preferred_element_type=F32)
             + qd * jnp.dot(q, st.astype(BF16), preferred_element_type=F32))
        kdec = (k.astype(F32) * kd).astype(BF16)
        s_scr[...] = gl * st + lax.dot_general(kdec, v, TN_DIMS, preferred_element_type=F32)
        a_ref[rows, :] = _head_norm_gate(o, sg_ref[rows, :])
        return carry

    lax.fori_loop(0, nc, body, 0)
    st_ref[0, 0] = s_scr[...]


def _ret_tables(length, valid):
    lg = jnp.log(1.0 - 2.0 ** (-5.0 - jnp.arange(H_RET, dtype=F32)))
    idx = jnp.arange(length, dtype=F32)
    diff = idx[:, None] - idx[None, :]
    ok = (diff >= 0) & (idx[None, :] < valid)
    dm = jnp.where(ok[None], jnp.exp(lg[:, None, None] * jnp.where(ok, diff, 0.0)[None]), 0.0)
    qd = jnp.exp(lg[:, None] * (idx[None, :] + 1.0))[..., None]
    kd = jnp.where(idx[None, :] < valid, jnp.exp(lg[:, None] * (valid - 1.0 - idx[None, :])), 0.0)[..., None]
    gl = jnp.exp(lg * valid)[:, None, None]
    return dm, qd, kd, gl


def _ret_prompt(z, *, batch, seq, chunk=RET_CHUNK):
    dm, qd, kd, gl = _ret_tables(chunk, chunk)
    cpb = 1024 // DK_RET
    kern = functools.partial(_ret_kernel, chunk=chunk)
    zspec = lambda sec: pl.BlockSpec((seq, DK_RET), lambda b, h: (b, sec * cpb + h))
    tab = lambda shape: pl.BlockSpec((1,) + shape, lambda b, h: (h, 0, 0))
    return pl.pallas_call(
        kern,
        grid=(batch, H_RET),
        in_specs=[zspec(ZB_QR), zspec(ZB_KR), zspec(ZB_VR), zspec(ZB_GR),
                  tab((chunk, chunk)), tab((chunk, 1)), tab((chunk, 1)), tab((1, 1))],
        out_specs=[pl.BlockSpec((seq, DV_RET), lambda b, h: (b, h)),
                   pl.BlockSpec((1, 1, DK_RET, DV_RET), lambda b, h: (b, h, 0, 0))],
        out_shape=[jax.ShapeDtypeStruct((batch * seq, W_RET), BF16),
                   jax.ShapeDtypeStruct((batch, H_RET, DK_RET, DV_RET), F32)],
        scratch_shapes=[pltpu.VMEM((DK_RET, DV_RET), F32)],
        compiler_params=_params(("arbitrary", "arbitrary")),
        name="ret_prompt",
    )(z, z, z, z, dm, qd, kd, gl)


def _ret_sample_kernel(q_ref, k_ref, v_ref, sg_ref, st_ref, dm_ref, qd_ref, kd_ref, gl_ref, a_ref, nst_ref):
    tp = q_ref.shape[1]
    pad = jnp.zeros((PAGE - tp, DK_RET), BF16)
    for h in range(H_RET):
        cols = slice(h * DK_RET, (h + 1) * DK_RET)
        q = q_ref[0, :, cols]
        k = k_ref[0, :, cols]
        v = v_ref[0, :, cols]
        kp = jnp.concatenate([k, pad], axis=0)
        vp = jnp.concatenate([v, pad], axis=0)
        kdp = jnp.concatenate([(k.astype(F32) * kd_ref[h]).astype(BF16), pad], axis=0)
        st = st_ref[0, h]
        sc = lax.dot_general(q, kp, NT_DIMS, preferred_element_type=F32) * dm_ref[h]
        o = (jnp.dot(sc.astype(BF16), vp, preferred_element_type=F32)
             + qd_ref[h] * jnp.dot(q, st.astype(BF16), preferred_element_type=F32))
        nst_ref[0, h] = gl_ref[h] * st + lax.dot_general(kdp, vp, TN_DIMS, preferred_element_type=F32)
        a_ref[0, :, cols] = _head_norm_gate(o, sg_ref[0, :, cols])


def _ret_sample(z3, state, *, valid):
    b, tp, _ = z3.shape
    dm, qd, kd, gl = _ret_tables(tp, valid)
    dm = jnp.pad(dm, ((0, 0), (0, 0), (0, PAGE - tp)))
    zspec = lambda sec: pl.BlockSpec((1, tp, 1024), lambda i: (i, 0, sec))
    full = lambda a: pl.BlockSpec(a.shape, lambda i: (0,) * a.ndim)
    st_spec = pl.BlockSpec((1, H_RET, DK_RET, DV_RET), lambda i: (i, 0, 0, 0))
    return pl.pallas_call(
        _ret_sample_kernel,
        grid=(b,),
        in_specs=[zspec(ZB_QR), zspec(ZB_KR), zspec(ZB_VR), zspec(ZB_GR), st_spec,
                  full(dm), full(qd), full(kd), full(gl)],
        out_specs=[pl.BlockSpec((1, tp, W_RET), lambda i: (i, 0, 0)), st_spec],
        out_shape=[jax.ShapeDtypeStruct((b, tp, W_RET), BF16),
                   jax.ShapeDtypeStruct(state.shape, F32)],
        compiler_params=_params(("arbitrary",)),
        name="ret_sample",
    )(z3, z3, z3, z3, state, dm, qd, kd, gl)


def _fox_sample_kernel(pt_ref, *refs, n_new, scale, pps):
    del pt_ref
    k_refs = refs[0:pps]
    v_refs = refs[pps:2 * pps]
    lf_refs = refs[2 * pps:3 * pps]
    q_ref, kn_ref, vn_ref, sg_ref, lfn_ref, o_ref = refs[3 * pps:3 * pps + 6]
    qbd_scr, m_scr, l_scr, acc_scr, carry_scr, g_scr = refs[3 * pps + 6:]
    p_id = pl.program_id(1)
    n_rows = n_new * H_FOX

    row = lax.broadcasted_iota(jnp.int32, (n_rows, W_FOX), 0)
    lane = lax.broadcasted_iota(jnp.int32, (n_rows, W_FOX), 1)
    own_head = (lane // HD_FOX) == (row % H_FOX)

    @pl.when(p_id == 0)
    def _():
        q = q_ref[0].astype(F32)
        qrep = jnp.concatenate([jnp.broadcast_to(q[t:t + 1, :], (H_FOX, W_FOX)) for t in range(n_new)], axis=0)
        qbd_scr[...] = jnp.where(own_head, qrep, 0.0).astype(BF16)
        m_scr[...] = jnp.full_like(m_scr, -jnp.inf)
        l_scr[...] = jnp.zeros_like(l_scr)
        acc_scr[...] = jnp.zeros_like(acc_scr)
        carry_scr[...] = jnp.zeros_like(carry_scr)
        lfn = lfn_ref[0]
        r8 = lax.broadcasted_iota(jnp.int32, (H_FOX, H_FOX), 0)
        c8 = lax.broadcasted_iota(jnp.int32, (H_FOX, H_FOX), 1)
        g = jnp.zeros((1, H_FOX), F32)
        cols = []
        for t in range(n_new):
            g = g + lfn[t:t + 1, :]
            cols.append(jnp.sum(jnp.where(r8 == c8, jnp.broadcast_to(g, (H_FOX, H_FOX)), 0.0), axis=1, keepdims=True))
        g_scr[...] = jnp.concatenate(cols, axis=0)

    qbd = qbd_scr[...]
    gcol = g_scr[...]

    lf_all = jnp.concatenate([r[0] for r in lf_refs], axis=0)
    nl = lf_all.shape[0]
    jj = lax.broadcasted_iota(jnp.int32, (PAGE, PAGE), 0)
    ss = lax.broadcasted_iota(jnp.int32, (PAGE, PAGE), 1)
    after = jnp.where(jj > ss, 1.0, 0.0).astype(BF16)
    parts = jnp.dot(jnp.concatenate(_split3(lf_all), axis=0), after, preferred_element_type=F32)
    r_loc = parts[0:nl] + parts[nl:2 * nl] + parts[2 * nl:3 * nl]
    tot = jnp.sum(lf_all, axis=1, keepdims=True)
    c = carry_scr[...]
    pieces = [None] * pps
    for r in reversed(range(pps)):
        pieces[r] = r_loc[r * H_FOX:(r + 1) * H_FOX, :] + c
        c = c + tot[r * H_FOX:(r + 1) * H_FOX, :]
    carry_scr[...] = c
    bias = jnp.concatenate([jnp.concatenate([pc] * n_new, axis=0) for pc in pieces], axis=1)

    s = jnp.concatenate(
        [lax.dot_general(qbd, kr[0].astype(BF16), NT_DIMS, preferred_element_type=F32) for kr in k_refs], axis=1)
    s = s * scale + gcol + bias
    m_old = m_scr[...]
    m_new = jnp.maximum(m_old, jnp.max(s, axis=1, keepdims=True))
    alpha = jnp.exp(m_old - m_new)
    p = jnp.exp(s - m_new)
    l_scr[...] = alpha * l_scr[...] + jnp.sum(p, axis=1, keepdims=True)
    pb = p.astype(BF16)
    acc = alpha * acc_scr[...]
    for r in range(pps):
        acc = acc + jnp.dot(pb[:, r * PAGE:(r + 1) * PAGE], v_refs[r][0].astype(BF16), preferred_element_type=F32)
    acc_scr[...] = acc
    m_scr[...] = m_new

    @pl.when(p_id == pl.num_programs(1) - 1)
    def _():
        qf = qbd_scr[...].astype(F32)
        kn = kn_ref[0].astype(F32)
        vn = vn_ref[0].astype(F32)
        g_col = g_scr[...]
        tok = lax.broadcasted_iota(jnp.int32, (n_rows, 1), 0) // H_FOX
        logits = []
        for t in range(n_new):
            sn = jnp.sum(qf * kn[t:t + 1, :], axis=1, keepdims=True) * scale
            g_t = jnp.concatenate([g_col[t * H_FOX:(t + 1) * H_FOX, :]] * n_new, axis=0)
            logits.append(jnp.where(tok >= t, sn + g_col - g_t, -jnp.inf))
        m0 = m_scr[...]
        m1 = m0
        for lg in logits:
            m1 = jnp.maximum(m1, lg)
        a1 = jnp.exp(m0 - m1)
        l1 = a1 * l_scr[...]
        acc1 = a1 * acc_scr[...]
        for t, lg in enumerate(logits):
            pt = jnp.exp(lg - m1)
            l1 = l1 + pt
            acc1 = acc1 + pt * vn[t:t + 1, :]
        o = jnp.where(own_head, acc1 / l1, 0.0)
        outs = [jnp.sum(o[t * H_FOX:(t + 1) * H_FOX, :], axis=0, keepdims=True) for t in range(n_new)]
        outs.append(jnp.zeros((o_ref.shape[1] - n_new, W_FOX), F32))
        o_ref[0] = (jnp.concatenate(outs, axis=0) * sg_ref[0].astype(F32)).astype(BF16)


def _fox_sample(z3, lf_new, cache_k, cache_v, cache_lf_t, page_table, *, n_new, pps=PAGES_PER_STEP):
    b, tp, _ = z3.shape
    n_pages = page_table.shape[1]
    steps = n_pages // pps
    n_rows = n_new * H_FOX
    pt_flat = page_table.reshape(-1)

    def page_map(r):
        return lambda i, p, pt: (pt[i * n_pages + (steps - 1 - p) * pps + r], 0, 0)

    kv_specs = [pl.BlockSpec((1, PAGE, W_FOX), page_map(r)) for r in range(pps)]
    lf_specs = [pl.BlockSpec((1, H_FOX, PAGE), page_map(r)) for r in range(pps)]
    zspec = lambda sec: pl.BlockSpec((1, tp, 1024), lambda i, p, pt: (i, 0, sec))
    kern = functools.partial(_fox_sample_kernel, n_new=n_new, scale=HD_FOX ** -0.5, pps=pps)
    grid_spec = pltpu.PrefetchScalarGridSpec(
        num_scalar_prefetch=1,
        grid=(b, steps),
        in_specs=kv_specs + kv_specs + lf_specs + [
            zspec(ZB_QF), zspec(ZB_KF), zspec(ZB_VF), zspec(ZB_GF),
            pl.BlockSpec((1, tp, H_FOX), lambda i, p, pt: (i, 0, 0)),
        ],
        out_specs=pl.BlockSpec((1, tp, W_FOX), lambda i, p, pt: (i, 0, 0)),
        scratch_shapes=[
            pltpu.VMEM((n_rows, W_FOX), BF16),
            pltpu.VMEM((n_rows, 1), F32),
            pltpu.VMEM((n_rows, 1), F32),
            pltpu.VMEM((n_rows, W_FOX), F32),
            pltpu.VMEM((H_FOX, 1), F32),
            pltpu.VMEM((n_rows, 1), F32),
        ],
    )
    return pl.pallas_call(
        kern,
        grid_spec=grid_spec,
        out_shape=jax.ShapeDtypeStruct((b, tp, W_FOX), BF16),
        compiler_params=_params(("arbitrary", "arbitrary")),
        name="fox_sample",
    )(pt_flat, *([cache_k] * pps), *([cache_v] * pps), *([cache_lf_t] * pps), z3, z3, z3, z3, lf_new)


def _merge_kernel(ar_ref, af_ref, mr_ref, mf_ref, x_ref, gate_ref, wb_ref, wo_ref, gf_ref, y_ref):
    p_r = jnp.dot(ar_ref[...], wb_ref[0:W_RET, :], preferred_element_type=F32)
    p_f = jnp.dot(af_ref[...], wb_ref[W_RET:W_RET + W_FOX, :], preferred_element_type=F32)
    merged = mr_ref[...].astype(F32) * p_r + mf_ref[...].astype(F32) * p_f
    out = x_ref[...] + gate_ref[0] * jnp.dot(merged.astype(BF16), wo_ref[...], preferred_element_type=F32)
    y = out * lax.rsqrt(jnp.mean(out * out, axis=-1, keepdims=True) + EPS)
    y_ref[...] = y * gf_ref[...]


def _merge(a_r, a_f, z, x2d, gate, w_branch, w_out, g_final, *, tm):
    rows, d = x2d.shape
    nrb = rows // tm
    groups, r, _ = gate.shape
    bpg = nrb // groups
    return pl.pallas_call(
        _merge_kernel,
        grid=(nrb,),
        in_specs=[
            pl.BlockSpec((tm, W_RET), lambda i: (i, 0)),
            pl.BlockSpec((tm, W_FOX), lambda i: (i, 0)),
            pl.BlockSpec((tm, d), lambda i: (i, 4)),
            pl.BlockSpec((tm, d), lambda i: (i, 5)),
            pl.BlockSpec((tm, d), lambda i: (i, 0)),
            pl.BlockSpec((1, r, d), lambda i: (i // bpg, 0, 0)),
            pl.BlockSpec(w_branch.shape, lambda i: (0, 0)),
            pl.BlockSpec(w_out.shape, lambda i: (0, 0)),
            pl.BlockSpec((1, d), lambda i: (0, 0)),
        ],
        out_specs=pl.BlockSpec((tm, d), lambda i: (i, 0)),
        out_shape=jax.ShapeDtypeStruct((rows, d), F32),
        compiler_params=_params(("arbitrary",)),
        name="merge",
    )(a_r, a_f, z, z, x2d, gate, w_branch, w_out, g_final.reshape(1, d))


def _rope_tables(pos):
    inv = 1.0 / (ROPE_BASE ** (jnp.arange(0, DK_RET, 2, dtype=F32) / DK_RET))
    ang = pos.astype(F32)[:, None] * inv[None, :]
    return jnp.cos(ang), jnp.sin(ang)


def kernel(x_prompt, x_sample, c_prompt, c_sample, cache_k, cache_v, cache_logf, state_ret, page_table,
           w_in, b_fgt, g_norm, w_ada, b_ada, w_branch, w_out, g_final):
    depth = w_in.shape[0]
    assert depth == 1, "the layer loop is written for the single-layer configuration"
    batch, seq, d = x_prompt.shape
    dec_batch, dec_seq, _ = x_sample.shape
    n_pages = page_table.shape[1]
    past_len = n_pages * PAGE
    l = 0

    w_l = w_in[l]
    w_main = jnp.concatenate([w_l[:, :F_COL0], w_l[:, F_COL0 + H_FOX:]], axis=1).astype(BF16)
    w_f = jnp.pad(w_l[:, F_COL0:F_COL0 + H_FOX], ((0, 0), (0, 128 - H_FOX))).astype(BF16)
    wb = w_branch[l].astype(BF16)
    wo = w_out[l].astype(BF16)

    mod = _mod(jnp.concatenate([c_prompt, c_sample], axis=0), w_ada[l], b_ada[l])
    shift, scale, gate = jnp.split(mod, 3, axis=-1)

    rows_p = batch * seq
    xp2 = x_prompt.reshape(rows_p, d)
    cos_p, sin_p = _rope_tables(jnp.arange(seq))
    tm_p = min(1024, seq)
    z_p, kf_p, vf_p, lf_p = _proj(xp2, scale[:batch, None, :], shift[:batch, None, :], g_norm[l],
                                  w_main, w_f, b_fgt[l], cos_p, sin_p, tm=tm_p)
    t_fox = min(512, seq)
    f_t = _cumsum_t(lf_p.reshape(batch, seq, H_FOX).transpose(0, 2, 1), chunk=min(256, seq))
    f_rows = f_t.transpose(0, 2, 1).reshape(rows_p, H_FOX)
    a_f_p = _fox_prompt(z_p, f_rows, f_t.reshape(batch * H_FOX, seq // t_fox, 1, t_fox),
                        batch=batch, seq=seq, t=t_fox)
    a_r_p, st_p = _ret_prompt(z_p, batch=batch, seq=seq)
    y_p = _merge(a_r_p, a_f_p, z_p, xp2, gate[:batch, None, :], wb, wo, g_final, tm=min(256, rows_p))

    rows_s = dec_batch * T_PAD
    xs2 = jnp.pad(x_sample, ((0, 0), (0, T_PAD - dec_seq), (0, 0))).reshape(rows_s, d)
    pos_s = past_len + (jnp.arange(rows_s) % T_PAD)
    cos_s, sin_s = _rope_tables(pos_s)
    tm_s = min(256, rows_s)
    rep = lambda a: jnp.repeat(a[batch:], T_PAD, axis=0).reshape(rows_s // tm_s, tm_s, d)
    z_s, kf_s, vf_s, lf_s = _proj(xs2, rep(scale), rep(shift), g_norm[l], w_main, w_f, b_fgt[l],
                                  cos_s, sin_s, tm=tm_s)
    z_s3 = z_s.reshape(dec_batch, T_PAD, Z_COLS)
    a_r_s, st_s = _ret_sample(z_s3, state_ret[l].astype(F32), valid=dec_seq)
    n_pool = cache_k.shape[1]
    a_f_s = _fox_sample(z_s3, lf_s.reshape(dec_batch, T_PAD, H_FOX),
                        cache_k[l].reshape(n_pool, PAGE, W_FOX), cache_v[l].reshape(n_pool, PAGE, W_FOX),
                        jnp.swapaxes(cache_logf[l], 1, 2), page_table, n_new=dec_seq)
    y_s = _merge(a_r_s.reshape(rows_s, W_RET), a_f_s.reshape(rows_s, W_FOX), z_s, xs2, rep(gate),
                 wb, wo, g_final, tm=tm_s)

    take = lambda a, tail: a.reshape((dec_batch, T_PAD) + tail)[:, :dec_seq]
    return (y_p.reshape(batch, seq, d),
            take(y_s, (d,)),
            kf_p.reshape(1, batch, seq, H_FOX, HD_FOX),
            vf_p.reshape(1, batch, seq, H_FOX, HD_FOX),
            lf_p.reshape(1, batch, seq, H_FOX),
            st_p[None],
            take(kf_s, (H_FOX, HD_FOX))[None],
            take(vf_s, (H_FOX, HD_FOX))[None],
            take(lf_s, (H_FOX,))[None],
            st_s[None])
```

```python
import functools

import jax
import jax.numpy as jnp
import numpy as np
from jax import lax
from jax.experimental import pallas as pl
from jax.experimental.pallas import tpu as pltpu

F32 = jnp.float32
BF16 = jnp.bfloat16

D_MODEL = 2048
H_RET, DK_RET, DV_RET = 4, 256, 256
H_FOX, HD_FOX = 8, 128
W_RET = H_RET * DV_RET
W_FOX = H_FOX * HD_FOX
RET_CHUNK = 128
PAGE = 128
ROPE_BASE = 10000.0
EPS = 1e-6
T_PAD = 16
PAGES_PER_STEP = 8
BIAS_PAGES_PER_STEP = 16
PROJ_ROW_CHUNK = 256
Z_COLS = 12288
F_COL0 = 8192

ZB_QR, ZB_KR, ZB_VR, ZB_GR, ZB_QF, ZB_KF, ZB_VF, ZB_GF = range(8)
VMEM_LIMIT = 56 * 1024 * 1024

NT_DIMS = (((1,), (1,)), ((), ()))
TN_DIMS = (((0,), (0,)), ((), ()))


def _params(sem):
    return pltpu.CompilerParams(dimension_semantics=sem, vmem_limit_bytes=VMEM_LIMIT)


def _split3(x):
    hi = x.astype(BF16)
    r1 = x - hi.astype(F32)
    mid = r1.astype(BF16)
    lo = (r1 - mid.astype(F32)).astype(BF16)
    return hi, mid, lo


def _mod_kernel(c_ref, w_ref, b_ref, o_ref):
    c = c_ref[...]
    sc = c * jax.nn.sigmoid(c)
    o_ref[...] = jnp.dot(sc, w_ref[...], preferred_element_type=F32,
                         precision=lax.Precision.HIGHEST) + b_ref[...]


def _mod(c, w_ada, b_ada, tn=512):
    n, d = c.shape
    cols = w_ada.shape[1]
    return pl.pallas_call(
        _mod_kernel,
        grid=(cols // tn,),
        in_specs=[pl.BlockSpec((n, d), lambda j: (0, 0)),
                  pl.BlockSpec((d, tn), lambda j: (0, j)),
                  pl.BlockSpec((1, tn), lambda j: (0, j))],
        out_specs=pl.BlockSpec((n, tn), lambda j: (0, j)),
        out_shape=jax.ShapeDtypeStruct((n, cols), F32),
        compiler_params=_params(("arbitrary",)),
        name="mod",
    )(c, w_ada, b_ada.reshape(1, cols))


def _proj_kernel(x_ref, scale_ref, shift_ref, gn_ref, w_ref, wf_ref, bf_ref, cos_ref, sin_ref,
                 z_ref, kf_ref, vf_ref, lf_ref, h_scr, *, tn, rc_rows):
    j = pl.program_id(1)
    per_kb = 1024 // tn

    @pl.when(j == 0)
    def _():
        x = x_ref[...]
        ms = jnp.mean(x * x, axis=-1, keepdims=True)
        xn = x * lax.rsqrt(ms + EPS) * gn_ref[...]
        h = xn * (1.0 + scale_ref[0]) + shift_ref[0]
        hb = h.astype(BF16)
        h_scr[...] = hb
        f = jnp.dot(hb, wf_ref[...], preferred_element_type=F32)
        lf_ref[...] = jax.nn.log_sigmoid(f[:, :H_FOX] + bf_ref[...])

    sec = j // per_kb
    tm = h_scr.shape[0]
    rc = min(rc_rows, tm)

    def run(epilogue):
        for c0 in range(0, tm, rc):
            rows = slice(c0, c0 + rc)
            epilogue(jnp.dot(h_scr[rows, :], w_ref[...], preferred_element_type=F32), rows)

    def rope_epilogue(acc, rows):
        c = cos_ref[rows, :]
        s = sin_ref[rows, :]
        sc = jnp.where(sec == ZB_KR, DK_RET ** -0.5, 1.0).astype(F32)
        for hh in range(tn // DK_RET):
            lo, mid, hi = hh * DK_RET, hh * DK_RET + DK_RET // 2, (hh + 1) * DK_RET
            x1 = acc[:, lo:mid]
            x2 = acc[:, mid:hi]
            z_ref[rows, lo:mid] = ((x1 * c - x2 * s) * sc).astype(BF16)
            z_ref[rows, mid:hi] = ((x1 * s + x2 * c) * sc).astype(BF16)

    def kf_epilogue(acc, rows):
        z_ref[rows, :] = acc.astype(BF16)
        kf_ref[rows, :] = acc

    def vf_epilogue(acc, rows):
        z_ref[rows, :] = acc.astype(BF16)
        vf_ref[rows, :] = acc

    is_silu = (sec == ZB_GR) | (sec == ZB_GF)
    is_sig = sec > ZB_GF
    ca = jnp.where(is_silu | is_sig, 0.0, 1.0).astype(F32)
    cb = jnp.where(is_silu, 1.0, 0.0).astype(F32)
    cc = jnp.where(is_sig, 1.0, 0.0).astype(F32)

    def gate_epilogue(acc, rows):
        z_ref[rows, :] = (acc * ca + jax.nn.sigmoid(acc) * (acc * cb + cc)).astype(BF16)

    is_rope = sec <= ZB_KR
    pl.when(is_rope)(lambda: run(rope_epilogue))
    pl.when(sec == ZB_KF)(lambda: run(kf_epilogue))
    pl.when(sec == ZB_VF)(lambda: run(vf_epilogue))
    pl.when(jnp.logical_not(is_rope) & (sec != ZB_KF) & (sec != ZB_VF))(lambda: run(gate_epilogue))


def _proj(x2d, scale, shift, g_norm, w_main, w_f, b_f, cos, sin, *, tm, tn=512):
    rows, d = x2d.shape
    nrb = rows // tm
    ncb = Z_COLS // tn
    per_kb = 1024 // tn
    groups = scale.shape[0]
    bpg = nrb // groups
    r = scale.shape[1]
    ntab = cos.shape[0] // tm
    kern = functools.partial(_proj_kernel, tn=tn, rc_rows=PROJ_ROW_CHUNK)
    kf0, vf0 = ZB_KF * per_kb, ZB_VF * per_kb
    return pl.pallas_call(
        kern,
        grid=(nrb, ncb),
        in_specs=[
            pl.BlockSpec((tm, d), lambda i, j: (i, 0)),
            pl.BlockSpec((1, r, d), lambda i, j: (i // bpg, 0, 0)),
            pl.BlockSpec((1, r, d), lambda i, j: (i // bpg, 0, 0)),
            pl.BlockSpec((1, d), lambda i, j: (0, 0)),
            pl.BlockSpec((d, tn), lambda i, j: (0, j)),
            pl.BlockSpec((d, 128), lambda i, j: (0, 0)),
            pl.BlockSpec((1, H_FOX), lambda i, j: (0, 0)),
            pl.BlockSpec((tm, 128), lambda i, j: (i % ntab, 0)),
            pl.BlockSpec((tm, 128), lambda i, j: (i % ntab, 0)),
        ],
        out_specs=[
            pl.BlockSpec((tm, tn), lambda i, j: (i, j)),
            pl.BlockSpec((tm, tn), lambda i, j: (i, jnp.clip(j - kf0, 0, per_kb - 1))),
            pl.BlockSpec((tm, tn), lambda i, j: (i, jnp.clip(j - vf0, 0, per_kb - 1))),
            pl.BlockSpec((tm, H_FOX), lambda i, j: (i, 0)),
        ],
        out_shape=[
            jax.ShapeDtypeStruct((rows, Z_COLS), BF16),
            jax.ShapeDtypeStruct((rows, W_FOX), F32),
            jax.ShapeDtypeStruct((rows, W_FOX), F32),
            jax.ShapeDtypeStruct((rows, H_FOX), F32),
        ],
        scratch_shapes=[pltpu.VMEM((tm, d), BF16)],
        compiler_params=_params(("arbitrary", "arbitrary")),
        name="proj",
    )(x2d, scale, shift, g_norm.reshape(1, d), w_main, w_f, b_f.reshape(1, H_FOX), cos, sin)


def _cumsum_kernel(lt_ref, ft_ref, *, chunk):
    s = lt_ref.shape[2]
    row = lax.broadcasted_iota(jnp.int32, (chunk, chunk), 0)
    col = lax.broadcasted_iota(jnp.int32, (chunk, chunk), 1)
    tri = jnp.where(row <= col, 1.0, 0.0).astype(BF16)
    carry = jnp.zeros((H_FOX, 1), F32)
    for c in range(s // chunk):
        lf = lt_ref[0, :, c * chunk:(c + 1) * chunk]
        terms = [t.astype(F32) for t in _split3(lf)] + [jnp.zeros_like(lf)]
        parts = jnp.dot(jnp.concatenate(terms, axis=0).astype(BF16), tri, preferred_element_type=F32)
        cs = parts[0:H_FOX] + parts[H_FOX:2 * H_FOX] + parts[2 * H_FOX:3 * H_FOX] + carry
        ft_ref[0, :, c * chunk:(c + 1) * chunk] = cs
        carry = cs[:, chunk - 1:chunk]


def _cumsum_t(lf_t, chunk=256):
    b, h, s = lf_t.shape
    return pl.pallas_call(
        functools.partial(_cumsum_kernel, chunk=chunk),
        grid=(b,),
        in_specs=[pl.BlockSpec((1, h, s), lambda i: (i, 0, 0))],
        out_specs=pl.BlockSpec((1, h, s), lambda i: (i, 0, 0)),
        out_shape=jax.ShapeDtypeStruct((b, h, s), F32),
        compiler_params=_params(("arbitrary",)),
        name="cumsum",
    )(lf_t)


def _fox_kernel(q_ref, k_ref, v_ref, sg_ref, f_ref, ft_ref, o_ref, *, t, scale):
    h = pl.program_id(1)
    qi = pl.program_id(2)
    q = q_ref[...]
    lane = lax.broadcasted_iota(jnp.int32, f_ref.shape, 1)
    fq = jnp.sum(jnp.where(lane == h, f_ref[...], 0.0), axis=1, keepdims=True)

    def block(kj, carry, masked):
        m, l, acc = carry
        rows = pl.ds(pl.multiple_of(kj * t, t), t)
        k = k_ref[rows, :]
        v = v_ref[rows, :]
        fk = ft_ref[0, kj]
        s = lax.dot_general(q, k, NT_DIMS, preferred_element_type=F32) * scale + fq - fk
        if masked:
            r = lax.broadcasted_iota(jnp.int32, (t, t), 0)
            c = lax.broadcasted_iota(jnp.int32, (t, t), 1)
            s = jnp.where(r >= c, s, -jnp.inf)
        m_new = jnp.maximum(m, jnp.max(s, axis=1, keepdims=True))
        alpha = jnp.exp(m - m_new)
        p = jnp.exp(s - m_new)
        l = alpha * l + jnp.sum(p, axis=1, keepdims=True)
        acc = alpha * acc + jnp.dot(p.astype(BF16), v, preferred_element_type=F32)
        return m_new, l, acc

    init = (jnp.full((t, 1), -jnp.inf, F32), jnp.zeros((t, 1), F32), jnp.zeros((t, HD_FOX), F32))
    carry = lax.fori_loop(0, qi, lambda kj, c: block(kj, c, False), init)
    m, l, acc = block(qi, carry, True)
    o_ref[...] = ((acc / l) * sg_ref[...].astype(F32)).astype(BF16)


def _fox_prompt(z, f_rows, f_t, *, batch, seq, t=512):
    nq = seq // t
    cpb = 1024 // HD_FOX
    kern = functools.partial(_fox_kernel, t=t, scale=HD_FOX ** -0.5)
    return pl.pallas_call(
        kern,
        grid=(batch, H_FOX, nq),
        in_specs=[
            pl.BlockSpec((t, HD_FOX), lambda b, h, i: (b * nq + i, ZB_QF * cpb + h)),
            pl.BlockSpec((seq, HD_FOX), lambda b, h, i: (b, ZB_KF * cpb + h)),
            pl.BlockSpec((seq, HD_FOX), lambda b, h, i: (b, ZB_VF * cpb + h)),
            pl.BlockSpec((t, HD_FOX), lambda b, h, i: (b * nq + i, ZB_GF * cpb + h)),
            pl.BlockSpec((t, H_FOX), lambda b, h, i: (b * nq + i, 0)),
            pl.BlockSpec((1, nq, 1, t), lambda b, h, i: (b * H_FOX + h, 0, 0, 0)),
        ],
        out_specs=pl.BlockSpec((t, HD_FOX), lambda b, h, i: (b * nq + i, h)),
        out_shape=jax.ShapeDtypeStruct((batch * seq, W_FOX), BF16),
        compiler_params=_params(("arbitrary", "arbitrary", "arbitrary")),
        name="fox_prompt",
    )(z, z, z, z, f_rows, f_t)


def _head_norm_gate(o, sg):
    on = o * lax.rsqrt(jnp.mean(o * o, axis=-1, keepdims=True) + EPS)
    return (on * sg.astype(F32)).astype(BF16)


def _ret_kernel(q_ref, k_ref, v_ref, sg_ref, dm_ref, qd_ref, kd_ref, gl_ref, a_ref, st_ref, s_scr, *, chunk):
    s_scr[...] = jnp.zeros_like(s_scr)
    dm = dm_ref[0]
    qd = qd_ref[0]
    kd = kd_ref[0]
    gl = gl_ref[0]
    nc = q_ref.shape[0] // chunk

    def body(c, carry):
        rows = pl.ds(pl.multiple_of(c * chunk, chunk), chunk)
        q = q_ref[rows, :]
        k = k_ref[rows, :]
        v = v_ref[rows, :]
        st = s_scr[...]
        sc = lax.dot_general(q, k, NT_DIMS, preferred_element_type=F32) * dm
        o = (jnp.dot(sc.astype(BF16), v, preferred_element_type=F32)
             + qd * jnp.dot(q, st.astype(BF16), preferred_element_type=F32))
        kdec = (k.astype(F32) * kd).astype(BF16)
        s_scr[...] = gl * st + lax.dot_general(kdec, v, TN_DIMS, preferred_element_type=F32)
        a_ref[rows, :] = _head_norm_gate(o, sg_ref[rows, :])
        return carry

    lax.fori_loop(0, nc, body, 0)
    st_ref[0, 0] = s_scr[...]


def _ret_tables(length, valid):
    lg = jnp.log(1.0 - 2.0 ** (-5.0 - jnp.arange(H_RET, dtype=F32)))
    idx = jnp.arange(length, dtype=F32)
    diff = idx[:, None] - idx[None, :]
    ok = (diff >= 0) & (idx[None, :] < valid)
    dm = jnp.where(ok[None], jnp.exp(lg[:, None, None] * jnp.where(ok, diff, 0.0)[None]), 0.0)
    qd = jnp.exp(lg[:, None] * (idx[None, :] + 1.0))[..., None]
    kd = jnp.where(idx[None, :] < valid, jnp.exp(lg[:, None] * (valid - 1.0 - idx[None, :])), 0.0)[..., None]
    gl = jnp.exp(lg * valid)[:, None, None]
    return dm, qd, kd, gl


def _ret_prompt(z, *, batch, seq, chunk=RET_CHUNK):
    dm, qd, kd, gl = _ret_tables(chunk, chunk)
    cpb = 1024 // DK_RET
    kern = functools.partial(_ret_kernel, chunk=chunk)
    zspec = lambda sec: pl.BlockSpec((seq, DK_RET), lambda b, h: (b, sec * cpb + h))
    tab = lambda shape: pl.BlockSpec((1,) + shape, lambda b, h: (h, 0, 0))
    return pl.pallas_call(
        kern,
        grid=(batch, H_RET),
        in_specs=[zspec(ZB_QR), zspec(ZB_KR), zspec(ZB_VR), zspec(ZB_GR),
                  tab((chunk, chunk)), tab((chunk, 1)), tab((chunk, 1)), tab((1, 1))],
        out_specs=[pl.BlockSpec((seq, DV_RET), lambda b, h: (b, h)),
                   pl.BlockSpec((1, 1, DK_RET, DV_RET), lambda b, h: (b, h, 0, 0))],
        out_shape=[jax.ShapeDtypeStruct((batch * seq, W_RET), BF16),
                   jax.ShapeDtypeStruct((batch, H_RET, DK_RET, DV_RET), F32)],
        scratch_shapes=[pltpu.VMEM((DK_RET, DV_RET), F32)],
        compiler_params=_params(("arbitrary", "arbitrary")),
        name="ret_prompt",
    )(z, z, z, z, dm, qd, kd, gl)


def _ret_sample_kernel(q_ref, k_ref, v_ref, sg_ref, st_ref, dm_ref, qd_ref, kd_ref, gl_ref, a_ref, nst_ref):
    tp = q_ref.shape[1]
    pad = jnp.zeros((PAGE - tp, DK_RET), BF16)
    for h in range(H_RET):
        cols = slice(h * DK_RET, (h + 1) * DK_RET)
        q = q_ref[0, :, cols]
        k = k_ref[0, :, cols]
        v = v_ref[0, :, cols]
        kp = jnp.concatenate([k, pad], axis=0)
        vp = jnp.concatenate([v, pad], axis=0)
        kdp = jnp.concatenate([(k.astype(F32) * kd_ref[h]).astype(BF16), pad], axis=0)
        st = st_ref[0, h]
        sc = lax.dot_general(q, kp, NT_DIMS, preferred_element_type=F32) * dm_ref[h]
        o = (jnp.dot(sc.astype(BF16), vp, preferred_element_type=F32)
             + qd_ref[h] * jnp.dot(q, st.astype(BF16), preferred_element_type=F32))
        nst_ref[0, h] = gl_ref[h] * st + lax.dot_general(kdp, vp, TN_DIMS, preferred_element_type=F32)
        a_ref[0, :, cols] = _head_norm_gate(o, sg_ref[0, :, cols])


def _ret_sample(z3, state, *, valid):
    b, tp, _ = z3.shape
    dm, qd, kd, gl = _ret_tables(tp, valid)
    dm = jnp.pad(dm, ((0, 0), (0, 0), (0, PAGE - tp)))
    zspec = lambda sec: pl.BlockSpec((1, tp, 1024), lambda i: (i, 0, sec))
    full = lambda a: pl.BlockSpec(a.shape, lambda i: (0,) * a.ndim)
    st_spec = pl.BlockSpec((1, H_RET, DK_RET, DV_RET), lambda i: (i, 0, 0, 0))
    return pl.pallas_call(
        _ret_sample_kernel,
        grid=(b,),
        in_specs=[zspec(ZB_QR), zspec(ZB_KR), zspec(ZB_VR), zspec(ZB_GR), st_spec,
                  full(dm), full(qd), full(kd), full(gl)],
        out_specs=[pl.BlockSpec((1, tp, W_RET), lambda i: (i, 0, 0)), st_spec],
        out_shape=[jax.ShapeDtypeStruct((b, tp, W_RET), BF16),
                   jax.ShapeDtypeStruct(state.shape, F32)],
        compiler_params=_params(("arbitrary",)),
        name="ret_sample",
    )(z3, z3, z3, z3, state, dm, qd, kd, gl)


def _fox_bias_kernel(pt_ref, *refs, pps):
    del pt_ref
    lf_refs = refs[:pps]
    o_ref, carry_scr = refs[pps], refs[pps + 1]

    @pl.when(pl.program_id(1) == 0)
    def _():
        carry_scr[...] = jnp.zeros_like(carry_scr)

    lf = jnp.concatenate([r[0] for r in lf_refs], axis=0)
    nl = lf.shape[0]
    jj = lax.broadcasted_iota(jnp.int32, (PAGE, 2 * PAGE), 0)
    ss = lax.broadcasted_iota(jnp.int32, (PAGE, 2 * PAGE), 1)
    same_head = (jj % H_FOX) == (ss % H_FOX)
    w = jnp.where(same_head & ((ss >= PAGE) | (jj // H_FOX > ss // H_FOX)), 1.0, 0.0).astype(BF16)
    parts = jnp.dot(jnp.concatenate(_split3(lf), axis=0), w, preferred_element_type=F32)
    both = parts[0:nl] + parts[nl:2 * nl] + parts[2 * nl:3 * nl]
    within = both[:, :PAGE]
    tot = both[:, PAGE:]
    ri = lax.broadcasted_iota(jnp.int32, (nl, PAGE), 0)
    ci = lax.broadcasted_iota(jnp.int32, (nl, PAGE), 1)
    later = jnp.where((ci > ri) & (ci < nl), 1.0, 0.0).astype(BF16)
    t3 = jnp.concatenate(_split3(tot), axis=1)
    if nl < PAGE:
        t3 = jnp.concatenate([t3, jnp.zeros((PAGE - nl, 3 * PAGE), BF16)], axis=0)
    cross3 = jnp.dot(later, t3, preferred_element_type=F32)
    cross = cross3[:, :PAGE] + cross3[:, PAGE:2 * PAGE] + cross3[:, 2 * PAGE:]
    carry = carry_scr[...]
    r = within + cross + carry
    carry_scr[...] = carry + jnp.sum(tot, axis=0, keepdims=True)
    rows_per_page = PAGE * H_FOX // PAGE
    for pg in range(pps):
        o_ref[0, pg] = r[pg * rows_per_page:(pg + 1) * rows_per_page, :]


def _fox_bias(cache_lf, page_table, *, pps):
    b, n_pages = page_table.shape
    steps = n_pages // pps
    rpp = cache_lf.shape[1]

    def page_map(r):
        return lambda i, p, pt: (pt[i * n_pages + (steps - 1 - p) * pps + r], 0, 0)

    grid_spec = pltpu.PrefetchScalarGridSpec(
        num_scalar_prefetch=1,
        grid=(b, steps),
        in_specs=[pl.BlockSpec((1, rpp, PAGE), page_map(r)) for r in range(pps)],
        out_specs=pl.BlockSpec((1, pps, rpp, PAGE), lambda i, p, pt: (i, steps - 1 - p, 0, 0)),
        scratch_shapes=[pltpu.VMEM((1, PAGE), F32)],
    )
    return pl.pallas_call(
        functools.partial(_fox_bias_kernel, pps=pps),
        grid_spec=grid_spec,
        out_shape=jax.ShapeDtypeStruct((b, n_pages, rpp, PAGE), F32),
        compiler_params=_params(("arbitrary", "arbitrary")),
        name="fox_bias",
    )(page_table.reshape(-1), *([cache_lf] * pps))


def _heads_to_rows(x_row):
    return jnp.concatenate([x_row[:, h * HD_FOX:(h + 1) * HD_FOX] for h in range(H_FOX)], axis=0)


def _fox_sample_kernel(pt_ref, *refs, n_new, scale, pps):
    del pt_ref
    k_refs = refs[0:pps]
    v_refs = refs[pps:2 * pps]
    r_ref, q_ref, kn_ref, vn_ref, sg_ref, lfn_ref, o_ref = refs[2 * pps:2 * pps + 7]
    q_scr, m_scr, l_scr, acc_scr, g_scr = refs[2 * pps + 7:]
    p_id = pl.program_id(1)
    n_rows = n_new * H_FOX
    n_keys = PAGE * H_FOX

    @pl.when(p_id == 0)
    def _():
        q = q_ref[0].astype(F32)
        q_scr[...] = jnp.concatenate([_heads_to_rows(q[t:t + 1, :]) for t in range(n_new)], axis=0).astype(BF16)
        m_scr[...] = jnp.full_like(m_scr, -jnp.inf)
        l_scr[...] = jnp.zeros_like(l_scr)
        acc_scr[...] = jnp.zeros_like(acc_scr)
        lfn = lfn_ref[0]
        r8 = lax.broadcasted_iota(jnp.int32, (H_FOX, H_FOX), 0)
        c8 = lax.broadcasted_iota(jnp.int32, (H_FOX, H_FOX), 1)
        g = jnp.zeros((1, H_FOX), F32)
        cols = []
        for t in range(n_new):
            g = g + lfn[t:t + 1, :]
            cols.append(jnp.sum(jnp.where(r8 == c8, jnp.broadcast_to(g, (H_FOX, H_FOX)), 0.0), axis=1, keepdims=True))
        g_scr[...] = jnp.concatenate(cols, axis=0)

    qm = q_scr[...]
    gcol = g_scr[...]
    row = lax.broadcasted_iota(jnp.int32, (n_rows, n_keys), 0)
    lane = lax.broadcasted_iota(jnp.int32, (n_rows, n_keys), 1)
    own_head = (lane % H_FOX) == (row % H_FOX)

    logits = []
    for r in range(pps):
        kp = k_refs[r][0].reshape(n_keys, HD_FOX).astype(BF16)
        s = lax.dot_general(qm, kp, NT_DIMS, preferred_element_type=F32)
        logits.append(jnp.where(own_head, s * scale + gcol + r_ref[0, r], -jnp.inf))
    m_old = m_scr[...]
    m_new = m_old
    for lg in logits:
        m_new = jnp.maximum(m_new, jnp.max(lg, axis=1, keepdims=True))
    alpha = jnp.exp(m_old - m_new)
    l = alpha * l_scr[...]
    acc = alpha * acc_scr[...]
    for r, lg in enumerate(logits):
        p = jnp.exp(lg - m_new)
        l = l + jnp.sum(p, axis=1, keepdims=True)
        vp = v_refs[r][0].reshape(n_keys, HD_FOX).astype(BF16)
        acc = acc + jnp.dot(p.astype(BF16), vp, preferred_element_type=F32)
    m_scr[...] = m_new
    l_scr[...] = l
    acc_scr[...] = acc

    @pl.when(p_id == pl.num_programs(1) - 1)
    def _():
        qf = q_scr[...].astype(F32)
        kn = kn_ref[0].astype(F32)
        vn = vn_ref[0].astype(F32)
        g_col = g_scr[...]
        tok = lax.broadcasted_iota(jnp.int32, (n_rows, 1), 0) // H_FOX
        tile = lambda x: jnp.concatenate([x] * n_new, axis=0)
        lgs = []
        for t in range(n_new):
            sn = jnp.sum(qf * tile(_heads_to_rows(kn[t:t + 1, :])), axis=1, keepdims=True) * scale
            g_t = tile(g_col[t * H_FOX:(t + 1) * H_FOX, :])
            lgs.append(jnp.where(tok >= t, sn + g_col - g_t, -jnp.inf))
        m0 = m_scr[...]
        m1 = m0
        for lg in lgs:
            m1 = jnp.maximum(m1, lg)
        a1 = jnp.exp(m0 - m1)
        l1 = a1 * l_scr[...]
        acc1 = a1 * acc_scr[...]
        for t, lg in enumerate(lgs):
            pt = jnp.exp(lg - m1)
            l1 = l1 + pt
            acc1 = acc1 + pt * tile(_heads_to_rows(vn[t:t + 1, :]))
        o = acc1 / l1
        outs = [jnp.concatenate([o[t * H_FOX + h:t * H_FOX + h + 1, :] for h in range(H_FOX)], axis=1)
                for t in range(n_new)]
        outs.append(jnp.zeros((o_ref.shape[1] - n_new, W_FOX), F32))
        o_ref[0] = (jnp.concatenate(outs, axis=0) * sg_ref[0].astype(F32)).astype(BF16)


def _fox_sample(z3, lf_new, cache_k, cache_v, bias_rows, page_table, *, n_new, pps):
    b, tp, _ = z3.shape
    n_pages = page_table.shape[1]
    steps = n_pages // pps
    n_rows = n_new * H_FOX

    def page_map(r):
        return lambda i, p, pt: (pt[i * n_pages + (steps - 1 - p) * pps + r], 0, 0, 0)

    kv_specs = [pl.BlockSpec((1, PAGE, H_FOX, HD_FOX), page_map(r)) for r in range(pps)]
    zspec = lambda sec: pl.BlockSpec((1, tp, 1024), lambda i, p, pt: (i, 0, sec))
    kern = functools.partial(_fox_sample_kernel, n_new=n_new, scale=HD_FOX ** -0.5, pps=pps)
    grid_spec = pltpu.PrefetchScalarGridSpec(
        num_scalar_prefetch=1,
        grid=(b, steps),
        in_specs=kv_specs + kv_specs + [
            pl.BlockSpec((1, pps, 1, PAGE * H_FOX), lambda i, p, pt: (i, steps - 1 - p, 0, 0)),
            zspec(ZB_QF), zspec(ZB_KF), zspec(ZB_VF), zspec(ZB_GF),
            pl.BlockSpec((1, tp, H_FOX), lambda i, p, pt: (i, 0, 0)),
        ],
        out_specs=pl.BlockSpec((1, tp, W_FOX), lambda i, p, pt: (i, 0, 0)),
        scratch_shapes=[
            pltpu.VMEM((n_rows, HD_FOX), BF16),
            pltpu.VMEM((n_rows, 1), F32),
            pltpu.VMEM((n_rows, 1), F32),
            pltpu.VMEM((n_rows, HD_FOX), F32),
            pltpu.VMEM((n_rows, 1), F32),
        ],
    )
    return pl.pallas_call(
        kern,
        grid_spec=grid_spec,
        out_shape=jax.ShapeDtypeStruct((b, tp, W_FOX), BF16),
        compiler_params=_params(("arbitrary", "arbitrary")),
        name="fox_sample",
    )(page_table.reshape(-1), *([cache_k] * pps), *([cache_v] * pps), bias_rows, z3, z3, z3, z3, lf_new)


def _merge_kernel(ar_ref, af_ref, mr_ref, mf_ref, x_ref, gate_ref, wb_ref, wo_ref, gf_ref, y_ref):
    p_r = jnp.dot(ar_ref[...], wb_ref[0:W_RET, :], preferred_element_type=F32)
    p_f = jnp.dot(af_ref[...], wb_ref[W_RET:W_RET + W_FOX, :], preferred_element_type=F32)
    merged = mr_ref[...].astype(F32) * p_r + mf_ref[...].astype(F32) * p_f
    out = x_ref[...] + gate_ref[0] * jnp.dot(merged.astype(BF16), wo_ref[...], preferred_element_type=F32)
    y = out * lax.rsqrt(jnp.mean(out * out, axis=-1, keepdims=True) + EPS)
    y_ref[...] = y * gf_ref[...]


def _merge(a_r, a_f, z, x2d, gate, w_branch, w_out, g_final, *, tm):
    rows, d = x2d.shape
    nrb = rows // tm
    groups, r, _ = gate.shape
    bpg = nrb // groups
    return pl.pallas_call(
        _merge_kernel,
        grid=(nrb,),
        in_specs=[
            pl.BlockSpec((tm, W_RET), lambda i: (i, 0)),
            pl.BlockSpec((tm, W_FOX), lambda i: (i, 0)),
            pl.BlockSpec((tm, d), lambda i: (i, 4)),
            pl.BlockSpec((tm, d), lambda i: (i, 5)),
            pl.BlockSpec((tm, d), lambda i: (i, 0)),
            pl.BlockSpec((1, r, d), lambda i: (i // bpg, 0, 0)),
            pl.BlockSpec(w_branch.shape, lambda i: (0, 0)),
            pl.BlockSpec(w_out.shape, lambda i: (0, 0)),
            pl.BlockSpec((1, d), lambda i: (0, 0)),
        ],
        out_specs=pl.BlockSpec((tm, d), lambda i: (i, 0)),
        out_shape=jax.ShapeDtypeStruct((rows, d), F32),
        compiler_params=_params(("arbitrary",)),
        name="merge",
    )(a_r, a_f, z, z, x2d, gate, w_branch, w_out, g_final.reshape(1, d))


def _rope_tables(pos):
    inv = 1.0 / (ROPE_BASE ** (jnp.arange(0, DK_RET, 2, dtype=F32) / DK_RET))
    ang = pos.astype(F32)[:, None] * inv[None, :]
    return jnp.cos(ang), jnp.sin(ang)


def kernel(x_prompt, x_sample, c_prompt, c_sample, cache_k, cache_v, cache_logf, state_ret, page_table,
           w_in, b_fgt, g_norm, w_ada, b_ada, w_branch, w_out, g_final):
    depth = w_in.shape[0]
    assert depth == 1, "the layer loop is written for the single-layer configuration"
    batch, seq, d = x_prompt.shape
    dec_batch, dec_seq, _ = x_sample.shape
    n_pages = page_table.shape[1]
    past_len = n_pages * PAGE
    l = 0

    w_l = w_in[l]
    w_main = jnp.concatenate([w_l[:, :F_COL0], w_l[:, F_COL0 + H_FOX:]], axis=1).astype(BF16)
    w_f = jnp.pad(w_l[:, F_COL0:F_COL0 + H_FOX], ((0, 0), (0, 128 - H_FOX))).astype(BF16)
    wb = w_branch[l].astype(BF16)
    wo = w_out[l].astype(BF16)

    mod = _mod(jnp.concatenate([c_prompt, c_sample], axis=0), w_ada[l], b_ada[l])
    shift, scale, gate = jnp.split(mod, 3, axis=-1)

    rows_p = batch * seq
    xp2 = x_prompt.reshape(rows_p, d)
    cos_p, sin_p = _rope_tables(jnp.arange(seq))
    tm_p = min(1024, seq)
    z_p, kf_p, vf_p, lf_p = _proj(xp2, scale[:batch, None, :], shift[:batch, None, :], g_norm[l],
                                  w_main, w_f, b_fgt[l], cos_p, sin_p, tm=tm_p)
    t_fox = min(512, seq)
    f_t = _cumsum_t(lf_p.reshape(batch, seq, H_FOX).transpose(0, 2, 1), chunk=min(256, seq))
    f_rows = f_t.transpose(0, 2, 1).reshape(rows_p, H_FOX)
    a_f_p = _fox_prompt(z_p, f_rows, f_t.reshape(batch * H_FOX, seq // t_fox, 1, t_fox),
                        batch=batch, seq=seq, t=t_fox)
    a_r_p, st_p = _ret_prompt(z_p, batch=batch, seq=seq)
    y_p = _merge(a_r_p, a_f_p, z_p, xp2, gate[:batch, None, :], wb, wo, g_final, tm=min(256, rows_p))

    rows_s = dec_batch * T_PAD
    xs2 = jnp.pad(x_sample, ((0, 0), (0, T_PAD - dec_seq), (0, 0))).reshape(rows_s, d)
    pos_s = past_len + (jnp.arange(rows_s) % T_PAD)
    cos_s, sin_s = _rope_tables(pos_s)
    tm_s = min(256, rows_s)
    rep = lambda a: jnp.repeat(a[batch:], T_PAD, axis=0).reshape(rows_s // tm_s, tm_s, d)
    z_s, kf_s, vf_s, lf_s = _proj(xs2, rep(scale), rep(shift), g_norm[l], w_main, w_f, b_fgt[l],
                                  cos_s, sin_s, tm=tm_s)
    z_s3 = z_s.reshape(dec_batch, T_PAD, Z_COLS)
    a_r_s, st_s = _ret_sample(z_s3, state_ret[l].astype(F32), valid=dec_seq)
    n_pool = cache_k.shape[1]
    pps = min(PAGES_PER_STEP, n_pages)
    bias = _fox_bias(cache_logf[l].reshape(n_pool, H_FOX, PAGE), page_table,
                     pps=min(BIAS_PAGES_PER_STEP, n_pages))
    a_f_s = _fox_sample(z_s3, lf_s.reshape(dec_batch, T_PAD, H_FOX), cache_k[l], cache_v[l],
                        bias.reshape(dec_batch, n_pages, 1, PAGE * H_FOX), page_table, n_new=dec_seq, pps=pps)
    y_s = _merge(a_r_s.reshape(rows_s, W_RET), a_f_s.reshape(rows_s, W_FOX), z_s, xs2, rep(gate),
                 wb, wo, g_final, tm=tm_s)

    take = lambda a, tail: a.reshape((dec_batch, T_PAD) + tail)[:, :dec_seq]
    return (y_p.reshape(batch, seq, d),
            take(y_s, (d,)),
            kf_p.reshape(1, batch, seq, H_FOX, HD_FOX),
            vf_p.reshape(1, batch, seq, H_FOX, HD_FOX),
            lf_p.reshape(1, batch, seq, H_FOX),
            st_p[None],
            take(kf_s, (H_FOX, HD_FOX))[None],
            take(vf_s, (H_FOX, HD_FOX))[None],
            take(lf_s, (H_FOX,))[None],
            st_s[None])
```

```python
import functools

import jax
import jax.numpy as jnp
import numpy as np
from jax import lax
from jax.experimental import pallas as pl
from jax.experimental.pallas import tpu as pltpu

F32 = jnp.float32
BF16 = jnp.bfloat16

D_MODEL = 2048
H_RET, DK_RET, DV_RET = 4, 256, 256
H_FOX, HD_FOX = 8, 128
W_RET = H_RET * DV_RET
W_FOX = H_FOX * HD_FOX
RET_CHUNK = 512
PAGE = 128
ROPE_BASE = 10000.0
EPS = 1e-6
LOG2E = 1.4426950408889634
T_PAD = 16
PAGES_PER_STEP = 8
BIAS_PAGES_PER_STEP = 16
PROJ_ROW_CHUNK = 256
PROJ_TN = 512
FOX_SUBBLOCKS = 1
Z_COLS = 12288
F_COL0 = 8192

ZB_QR, ZB_KR, ZB_VR, ZB_GR, ZB_QF, ZB_KF, ZB_VF, ZB_GF = range(8)
VMEM_LIMIT = 56 * 1024 * 1024

NT_DIMS = (((1,), (1,)), ((), ()))
TN_DIMS = (((0,), (0,)), ((), ()))


def _params(sem):
    return pltpu.CompilerParams(dimension_semantics=sem, vmem_limit_bytes=VMEM_LIMIT)


def _split3(x):
    hi = x.astype(BF16)
    r1 = x - hi.astype(F32)
    mid = r1.astype(BF16)
    lo = (r1 - mid.astype(F32)).astype(BF16)
    return hi, mid, lo


def _mod_kernel(c_ref, w_ref, b_ref, o_ref):
    c = c_ref[...]
    sc = c * jax.nn.sigmoid(c)
    o_ref[...] = jnp.dot(sc, w_ref[...], preferred_element_type=F32,
                         precision=lax.Precision.HIGHEST) + b_ref[...]


def _mod(c, w_ada, b_ada, tn=1024):
    n, d = c.shape
    cols = w_ada.shape[1]
    return pl.pallas_call(
        _mod_kernel,
        grid=(cols // tn,),
        in_specs=[pl.BlockSpec((n, d), lambda j: (0, 0)),
                  pl.BlockSpec((d, tn), lambda j: (0, j)),
                  pl.BlockSpec((1, tn), lambda j: (0, j))],
        out_specs=pl.BlockSpec((n, tn), lambda j: (0, j)),
        out_shape=jax.ShapeDtypeStruct((n, cols), F32),
        compiler_params=_params(("arbitrary",)),
        name="mod",
    )(c, w_ada, b_ada.reshape(1, cols))


def _proj_kernel(x_ref, scale_ref, shift_ref, gn_ref, w_ref, wf_ref, bf_ref, cos_ref, sin_ref,
                 z_ref, kf_ref, vf_ref, lf_ref, h_scr, *, tn, rc_rows):
    j = pl.program_id(1)
    per_kb = 1024 // tn

    @pl.when(j == 0)
    def _():
        x = x_ref[...]
        ms = jnp.mean(x * x, axis=-1, keepdims=True)
        xn = x * lax.rsqrt(ms + EPS) * gn_ref[...]
        h = xn * (1.0 + scale_ref[0]) + shift_ref[0]
        hb = h.astype(BF16)
        h_scr[...] = hb
        f = jnp.dot(hb, wf_ref[...], preferred_element_type=F32)
        lf_ref[...] = jax.nn.log_sigmoid(f[:, :H_FOX] + bf_ref[...])

    sec = j // per_kb
    tm = h_scr.shape[0]
    rc = min(rc_rows, tm)

    def run(epilogue):
        for c0 in range(0, tm, rc):
            rows = slice(c0, c0 + rc)
            epilogue(jnp.dot(h_scr[rows, :], w_ref[0], preferred_element_type=F32), rows)

    def rope_epilogue(acc, rows):
        c = cos_ref[rows, :]
        s = sin_ref[rows, :]
        sc = jnp.where(sec == ZB_KR, DK_RET ** -0.5, 1.0).astype(F32)
        for hh in range(tn // DK_RET):
            lo, mid, hi = hh * DK_RET, hh * DK_RET + DK_RET // 2, (hh + 1) * DK_RET
            x1 = acc[:, lo:mid]
            x2 = acc[:, mid:hi]
            z_ref[rows, lo:mid] = ((x1 * c - x2 * s) * sc).astype(BF16)
            z_ref[rows, mid:hi] = ((x1 * s + x2 * c) * sc).astype(BF16)

    def kf_epilogue(acc, rows):
        z_ref[rows, :] = acc.astype(BF16)
        kf_ref[rows, :] = acc

    def vf_epilogue(acc, rows):
        z_ref[rows, :] = acc.astype(BF16)
        vf_ref[rows, :] = acc

    is_silu = (sec == ZB_GR) | (sec == ZB_GF)
    is_sig = sec > ZB_GF
    ca = jnp.where(is_silu | is_sig, 0.0, 1.0).astype(F32)
    cb = jnp.where(is_silu, 1.0, 0.0).astype(F32)
    cc = jnp.where(is_sig, 1.0, 0.0).astype(F32)

    def gate_epilogue(acc, rows):
        z_ref[rows, :] = (acc * ca + jax.nn.sigmoid(acc) * (acc * cb + cc)).astype(BF16)

    is_rope = sec <= ZB_KR
    pl.when(is_rope)(lambda: run(rope_epilogue))
    pl.when(sec == ZB_KF)(lambda: run(kf_epilogue))
    pl.when(sec == ZB_VF)(lambda: run(vf_epilogue))
    pl.when(jnp.logical_not(is_rope) & (sec != ZB_KF) & (sec != ZB_VF))(lambda: run(gate_epilogue))


def _proj(x2d, scale, shift, g_norm, w_main, w_f, b_f, cos, sin, *, tm):
    rows, d = x2d.shape
    tn = w_main.shape[2]
    nrb = rows // tm
    ncb = Z_COLS // tn
    per_kb = 1024 // tn
    groups = scale.shape[0]
    bpg = nrb // groups
    r = scale.shape[1]
    ntab = cos.shape[0] // tm
    kern = functools.partial(_proj_kernel, tn=tn, rc_rows=PROJ_ROW_CHUNK)
    kf0, vf0 = ZB_KF * per_kb, ZB_VF * per_kb
    return pl.pallas_call(
        kern,
        grid=(nrb, ncb),
        in_specs=[
            pl.BlockSpec((tm, d), lambda i, j: (i, 0)),
            pl.BlockSpec((1, r, d), lambda i, j: (i // bpg, 0, 0)),
            pl.BlockSpec((1, r, d), lambda i, j: (i // bpg, 0, 0)),
            pl.BlockSpec((1, d), lambda i, j: (0, 0)),
            pl.BlockSpec((1, d, tn), lambda i, j: (j, 0, 0)),
            pl.BlockSpec((d, 128), lambda i, j: (0, 0)),
            pl.BlockSpec((1, H_FOX), lambda i, j: (0, 0)),
            pl.BlockSpec((tm, 128), lambda i, j: (i % ntab, 0)),
            pl.BlockSpec((tm, 128), lambda i, j: (i % ntab, 0)),
        ],
        out_specs=[
            pl.BlockSpec((tm, tn), lambda i, j: (i, j)),
            pl.BlockSpec((tm, tn), lambda i, j: (i, jnp.clip(j - kf0, 0, per_kb - 1))),
            pl.BlockSpec((tm, tn), lambda i, j: (i, jnp.clip(j - vf0, 0, per_kb - 1))),
            pl.BlockSpec((tm, H_FOX), lambda i, j: (i, 0)),
        ],
        out_shape=[
            jax.ShapeDtypeStruct((rows, Z_COLS), BF16),
            jax.ShapeDtypeStruct((rows, W_FOX), F32),
            jax.ShapeDtypeStruct((rows, W_FOX), F32),
            jax.ShapeDtypeStruct((rows, H_FOX), F32),
        ],
        scratch_shapes=[pltpu.VMEM((tm, d), BF16)],
        compiler_params=_params(("arbitrary", "arbitrary")),
        name="proj",
    )(x2d, scale, shift, g_norm.reshape(1, d), w_main, w_f, b_f.reshape(1, H_FOX), cos, sin)


def _cumsum_kernel(lt_ref, ft_ref, *, chunk):
    s = lt_ref.shape[2]
    row = lax.broadcasted_iota(jnp.int32, (chunk, chunk), 0)
    col = lax.broadcasted_iota(jnp.int32, (chunk, chunk), 1)
    tri = jnp.where(row <= col, 1.0, 0.0).astype(BF16)
    carry = jnp.zeros((H_FOX, 1), F32)
    for c in range(s // chunk):
        lf = lt_ref[0, :, c * chunk:(c + 1) * chunk]
        terms = [t.astype(F32) for t in _split3(lf)] + [jnp.zeros_like(lf)]
        parts = jnp.dot(jnp.concatenate(terms, axis=0).astype(BF16), tri, preferred_element_type=F32)
        cs = parts[0:H_FOX] + parts[H_FOX:2 * H_FOX] + parts[2 * H_FOX:3 * H_FOX] + carry
        ft_ref[0, :, c * chunk:(c + 1) * chunk] = cs
        carry = cs[:, chunk - 1:chunk]


def _cumsum_t(lf_t, chunk=256):
    b, h, s = lf_t.shape
    return pl.pallas_call(
        functools.partial(_cumsum_kernel, chunk=chunk),
        grid=(b,),
        in_specs=[pl.BlockSpec((1, h, s), lambda i: (i, 0, 0))],
        out_specs=pl.BlockSpec((1, h, s), lambda i: (i, 0, 0)),
        out_shape=jax.ShapeDtypeStruct((b, h, s), F32),
        compiler_params=_params(("arbitrary",)),
        name="cumsum",
    )(lf_t)


def _fox_kernel(q_ref, k_ref, v_ref, sg_ref, ft_ref, o_ref, *, t, scale, nsub):
    qi = pl.program_id(2)
    c2 = scale * LOG2E
    rs = t // nsub

    def block(kj, carry, masked):
        rows = pl.ds(pl.multiple_of(kj * t, t), t)
        k = k_ref[rows, :]
        v = v_ref[rows, :]
        fk2 = ft_ref[0, kj] * LOG2E
        out = []
        for sub, (m, l, acc) in enumerate(carry):
            q = q_ref[sub * rs:(sub + 1) * rs, :]
            s = lax.dot_general(q, k, NT_DIMS, preferred_element_type=F32) * c2 - fk2
            if masked:
                r = lax.broadcasted_iota(jnp.int32, (rs, t), 0) + sub * rs
                c = lax.broadcasted_iota(jnp.int32, (rs, t), 1)
                s = jnp.where(r >= c, s, -jnp.inf)
            m_new = jnp.maximum(m, jnp.max(s, axis=1, keepdims=True))
            alpha = jnp.exp2(m - m_new)
            p = jnp.exp2(s - m_new)
            l = alpha * l + jnp.sum(p, axis=1, keepdims=True)
            acc = alpha * acc + jnp.dot(p.astype(BF16), v, preferred_element_type=F32)
            out.append((m_new, l, acc))
        return tuple(out)

    init = tuple((jnp.full((rs, 1), -jnp.inf, F32), jnp.zeros((rs, 1), F32), jnp.zeros((rs, HD_FOX), F32))
                 for _ in range(nsub))
    carry = lax.fori_loop(0, qi, lambda kj, c: block(kj, c, False), init)
    for sub, (m, l, acc) in enumerate(block(qi, carry, True)):
        rows = slice(sub * rs, (sub + 1) * rs)
        o_ref[rows, :] = ((acc / l) * sg_ref[rows, :].astype(F32)).astype(BF16)


def _fox_prompt(z, f_t, *, batch, seq, t=512):
    nq = seq // t
    cpb = 1024 // HD_FOX
    kern = functools.partial(_fox_kernel, t=t, scale=HD_FOX ** -0.5, nsub=min(FOX_SUBBLOCKS, t // 128))
    return pl.pallas_call(
        kern,
        grid=(batch, H_FOX, nq),
        in_specs=[
            pl.BlockSpec((t, HD_FOX), lambda b, h, i: (b * nq + i, ZB_QF * cpb + h)),
            pl.BlockSpec((seq, HD_FOX), lambda b, h, i: (b, ZB_KF * cpb + h)),
            pl.BlockSpec((seq, HD_FOX), lambda b, h, i: (b, ZB_VF * cpb + h)),
            pl.BlockSpec((t, HD_FOX), lambda b, h, i: (b * nq + i, ZB_GF * cpb + h)),
            pl.BlockSpec((1, nq, 1, t), lambda b, h, i: (b * H_FOX + h, 0, 0, 0)),
        ],
        out_specs=pl.BlockSpec((t, HD_FOX), lambda b, h, i: (b * nq + i, h)),
        out_shape=jax.ShapeDtypeStruct((batch * seq, W_FOX), BF16),
        compiler_params=_params(("arbitrary", "arbitrary", "arbitrary")),
        name="fox_prompt",
    )(z, z, z, z, f_t)


def _head_norm_gate(o, sg):
    on = o * lax.rsqrt(jnp.mean(o * o, axis=-1, keepdims=True) + EPS)
    return (on * sg.astype(F32)).astype(BF16)


def _ret_kernel(q_ref, k_ref, v_ref, sg_ref, dm_ref, qd_ref, kd_ref, gl_ref, a_ref, st_ref, s_scr, *, chunk):
    s_scr[...] = jnp.zeros_like(s_scr)
    dm = dm_ref[0]
    qd = qd_ref[0]
    kd = kd_ref[0]
    gl = gl_ref[0]
    nc = q_ref.shape[0] // chunk

    def body(c, carry):
        rows = pl.ds(pl.multiple_of(c * chunk, chunk), chunk)
        q = q_ref[rows, :]
        k = k_ref[rows, :]
        v = v_ref[rows, :]
        st = s_scr[...]
        sc = lax.dot_general(q, k, NT_DIMS, preferred_element_type=F32) * dm
        o = (jnp.dot(sc.astype(BF16), v, preferred_element_type=F32)
             + qd * jnp.dot(q, st.astype(BF16), preferred_element_type=F32))
        kdec = (k.astype(F32) * kd).astype(BF16)
        s_scr[...] = gl * st + lax.dot_general(kdec, v, TN_DIMS, preferred_element_type=F32)
        a_ref[rows, :] = _head_norm_gate(o, sg_ref[rows, :])
        return carry

    lax.fori_loop(0, nc, body, 0)
    st_ref[0, 0] = s_scr[...]


def _ret_tables(length, valid):
    lg = jnp.log(1.0 - 2.0 ** (-5.0 - jnp.arange(H_RET, dtype=F32)))
    idx = jnp.arange(length, dtype=F32)
    diff = idx[:, None] - idx[None, :]
    ok = (diff >= 0) & (idx[None, :] < valid)
    dm = jnp.where(ok[None], jnp.exp(lg[:, None, None] * jnp.where(ok, diff, 0.0)[None]), 0.0)
    qd = jnp.exp(lg[:, None] * (idx[None, :] + 1.0))[..., None]
    kd = jnp.where(idx[None, :] < valid, jnp.exp(lg[:, None] * (valid - 1.0 - idx[None, :])), 0.0)[..., None]
    gl = jnp.exp(lg * valid)[:, None, None]
    return dm, qd, kd, gl


def _ret_prompt(z, *, batch, seq, chunk=RET_CHUNK):
    dm, qd, kd, gl = _ret_tables(chunk, chunk)
    cpb = 1024 // DK_RET
    kern = functools.partial(_ret_kernel, chunk=chunk)
    zspec = lambda sec: pl.BlockSpec((seq, DK_RET), lambda b, h: (b, sec * cpb + h))
    tab = lambda shape: pl.BlockSpec((1,) + shape, lambda b, h: (h, 0, 0))
    return pl.pallas_call(
        kern,
        grid=(batch, H_RET),
        in_specs=[zspec(ZB_QR), zspec(ZB_KR), zspec(ZB_VR), zspec(ZB_GR),
                  tab((chunk, chunk)), tab((chunk, 1)), tab((chunk, 1)), tab((1, 1))],
        out_specs=[pl.BlockSpec((seq, DV_RET), lambda b, h: (b, h)),
                   pl.BlockSpec((1, 1, DK_RET, DV_RET), lambda b, h: (b, h, 0, 0))],
        out_shape=[jax.ShapeDtypeStruct((batch * seq, W_RET), BF16),
                   jax.ShapeDtypeStruct((batch, H_RET, DK_RET, DV_RET), F32)],
        scratch_shapes=[pltpu.VMEM((DK_RET, DV_RET), F32)],
        compiler_params=_params(("arbitrary", "arbitrary")),
        name="ret_prompt",
    )(z, z, z, z, dm, qd, kd, gl)


def _ret_sample_kernel(q_ref, k_ref, v_ref, sg_ref, st_ref, dm_ref, qd_ref, kd_ref, gl_ref, a_ref, nst_ref):
    tp = q_ref.shape[1]
    pad = jnp.zeros((PAGE - tp, DK_RET), BF16)
    for h in range(H_RET):
        cols = slice(h * DK_RET, (h + 1) * DK_RET)
        q = q_ref[0, :, cols]
        k = k_ref[0, :, cols]
        v = v_ref[0, :, cols]
        kp = jnp.concatenate([k, pad], axis=0)
        vp = jnp.concatenate([v, pad], axis=0)
        kdp = jnp.concatenate([(k.astype(F32) * kd_ref[h]).astype(BF16), pad], axis=0)
        st = st_ref[0, h]
        sc = lax.dot_general(q, kp, NT_DIMS, preferred_element_type=F32) * dm_ref[h]
        o = (jnp.dot(sc.astype(BF16), vp, preferred_element_type=F32)
             + qd_ref[h] * jnp.dot(q, st.astype(BF16), preferred_element_type=F32))
        nst_ref[0, h] = gl_ref[h] * st + lax.dot_general(kdp, vp, TN_DIMS, preferred_element_type=F32)
        a_ref[0, :, cols] = _head_norm_gate(o, sg_ref[0, :, cols])


def _ret_sample(z3, state, *, valid):
    b, tp, _ = z3.shape
    dm, qd, kd, gl = _ret_tables(tp, valid)
    dm = jnp.pad(dm, ((0, 0), (0, 0), (0, PAGE - tp)))
    zspec = lambda sec: pl.BlockSpec((1, tp, 1024), lambda i: (i, 0, sec))
    full = lambda a: pl.BlockSpec(a.shape, lambda i: (0,) * a.ndim)
    st_spec = pl.BlockSpec((1, H_RET, DK_RET, DV_RET), lambda i: (i, 0, 0, 0))
    return pl.pallas_call(
        _ret_sample_kernel,
        grid=(b,),
        in_specs=[zspec(ZB_QR), zspec(ZB_KR), zspec(ZB_VR), zspec(ZB_GR), st_spec,
                  full(dm), full(qd), full(kd), full(gl)],
        out_specs=[pl.BlockSpec((1, tp, W_RET), lambda i: (i, 0, 0)), st_spec],
        out_shape=[jax.ShapeDtypeStruct((b, tp, W_RET), BF16),
                   jax.ShapeDtypeStruct(state.shape, F32)],
        compiler_params=_params(("arbitrary",)),
        name="ret_sample",
    )(z3, z3, z3, z3, state, dm, qd, kd, gl)


def _fox_bias_kernel(pt_ref, *refs, pps):
    del pt_ref
    lf_refs = refs[:pps]
    o_ref, carry_scr = refs[pps], refs[pps + 1]

    @pl.when(pl.program_id(1) == 0)
    def _():
        carry_scr[...] = jnp.zeros_like(carry_scr)

    lf = jnp.concatenate([r[0] for r in lf_refs], axis=0)
    nl = lf.shape[0]
    jj = lax.broadcasted_iota(jnp.int32, (PAGE, 2 * PAGE), 0)
    ss = lax.broadcasted_iota(jnp.int32, (PAGE, 2 * PAGE), 1)
    same_head = (jj % H_FOX) == (ss % H_FOX)
    w = jnp.where(same_head & ((ss >= PAGE) | (jj // H_FOX > ss // H_FOX)), 1.0, 0.0).astype(BF16)
    parts = jnp.dot(jnp.concatenate(_split3(lf), axis=0), w, preferred_element_type=F32)
    both = parts[0:nl] + parts[nl:2 * nl] + parts[2 * nl:3 * nl]
    within = both[:, :PAGE]
    tot = both[:, PAGE:]
    ri = lax.broadcasted_iota(jnp.int32, (nl, PAGE), 0)
    ci = lax.broadcasted_iota(jnp.int32, (nl, PAGE), 1)
    later = jnp.where((ci > ri) & (ci < nl), 1.0, 0.0).astype(BF16)
    t3 = jnp.concatenate(_split3(tot), axis=1)
    if nl < PAGE:
        t3 = jnp.concatenate([t3, jnp.zeros((PAGE - nl, 3 * PAGE), BF16)], axis=0)
    cross3 = jnp.dot(later, t3, preferred_element_type=F32)
    cross = cross3[:, :PAGE] + cross3[:, PAGE:2 * PAGE] + cross3[:, 2 * PAGE:]
    carry = carry_scr[...]
    r = within + cross + carry
    carry_scr[...] = carry + jnp.sum(tot, axis=0, keepdims=True)
    rows_per_page = PAGE * H_FOX // PAGE
    for pg in range(pps):
        o_ref[0, pg] = r[pg * rows_per_page:(pg + 1) * rows_per_page, :]


def _fox_bias(cache_lf, page_table, *, pps):
    b, n_pages = page_table.shape
    steps = n_pages // pps
    rpp = cache_lf.shape[1]

    def page_map(r):
        return lambda i, p, pt: (pt[i * n_pages + (steps - 1 - p) * pps + r], 0, 0)

    grid_spec = pltpu.PrefetchScalarGridSpec(
        num_scalar_prefetch=1,
        grid=(b, steps),
        in_specs=[pl.BlockSpec((1, rpp, PAGE), page_map(r)) for r in range(pps)],
        out_specs=pl.BlockSpec((1, pps, rpp, PAGE), lambda i, p, pt: (i, steps - 1 - p, 0, 0)),
        scratch_shapes=[pltpu.VMEM((1, PAGE), F32)],
    )
    return pl.pallas_call(
        functools.partial(_fox_bias_kernel, pps=pps),
        grid_spec=grid_spec,
        out_shape=jax.ShapeDtypeStruct((b, n_pages, rpp, PAGE), F32),
        compiler_params=_params(("arbitrary", "arbitrary")),
        name="fox_bias",
    )(page_table.reshape(-1), *([cache_lf] * pps))


def _heads_to_rows(x_row):
    return jnp.concatenate([x_row[:, h * HD_FOX:(h + 1) * HD_FOX] for h in range(H_FOX)], axis=0)


def _fox_sample_kernel(pt_ref, *refs, n_new, scale, pps):
    del pt_ref
    k_refs = refs[0:pps]
    v_refs = refs[pps:2 * pps]
    r_ref, q_ref, kn_ref, vn_ref, sg_ref, lfn_ref, o_ref = refs[2 * pps:2 * pps + 7]
    q_scr, m_scr, l_scr, acc_scr, g_scr = refs[2 * pps + 7:]
    p_id = pl.program_id(1)
    n_rows = n_new * H_FOX
    n_keys = PAGE * H_FOX

    @pl.when(p_id == 0)
    def _():
        q = q_ref[0].astype(F32)
        q_scr[...] = jnp.concatenate([_heads_to_rows(q[t:t + 1, :]) for t in range(n_new)], axis=0).astype(BF16)
        m_scr[...] = jnp.full_like(m_scr, -jnp.inf)
        l_scr[...] = jnp.zeros_like(l_scr)
        acc_scr[...] = jnp.zeros_like(acc_scr)
        lfn = lfn_ref[0]
        r8 = lax.broadcasted_iota(jnp.int32, (H_FOX, H_FOX), 0)
        c8 = lax.broadcasted_iota(jnp.int32, (H_FOX, H_FOX), 1)
        g = jnp.zeros((1, H_FOX), F32)
        cols = []
        for t in range(n_new):
            g = g + lfn[t:t + 1, :]
            cols.append(jnp.sum(jnp.where(r8 == c8, jnp.broadcast_to(g, (H_FOX, H_FOX)), 0.0), axis=1, keepdims=True))
        g_scr[...] = jnp.concatenate(cols, axis=0)

    qm = q_scr[...]
    gcol = g_scr[...]
    row = lax.broadcasted_iota(jnp.int32, (n_rows, n_keys), 0)
    lane = lax.broadcasted_iota(jnp.int32, (n_rows, n_keys), 1)
    own_head = (lane % H_FOX) == (row % H_FOX)

    logits = []
    for r in range(pps):
        kp = k_refs[r][0].reshape(n_keys, HD_FOX).astype(BF16)
        s = lax.dot_general(qm, kp, NT_DIMS, preferred_element_type=F32)
        logits.append(jnp.where(own_head, s * scale + gcol + r_ref[0, r], -jnp.inf))
    m_old = m_scr[...]
    m_new = m_old
    for lg in logits:
        m_new = jnp.maximum(m_new, jnp.max(lg, axis=1, keepdims=True))
    alpha = jnp.exp(m_old - m_new)
    l = alpha * l_scr[...]
    acc = alpha * acc_scr[...]
    for r, lg in enumerate(logits):
        p = jnp.exp(lg - m_new)
        l = l + jnp.sum(p, axis=1, keepdims=True)
        vp = v_refs[r][0].reshape(n_keys, HD_FOX).astype(BF16)
        acc = acc + jnp.dot(p.astype(BF16), vp, preferred_element_type=F32)
    m_scr[...] = m_new
    l_scr[...] = l
    acc_scr[...] = acc

    @pl.when(p_id == pl.num_programs(1) - 1)
    def _():
        qf = q_scr[...].astype(F32)
        kn = kn_ref[0].astype(F32)
        vn = vn_ref[0].astype(F32)
        g_col = g_scr[...]
        tok = lax.broadcasted_iota(jnp.int32, (n_rows, 1), 0) // H_FOX
        tile = lambda x: jnp.concatenate([x] * n_new, axis=0)
        lgs = []
        for t in range(n_new):
            sn = jnp.sum(qf * tile(_heads_to_rows(kn[t:t + 1, :])), axis=1, keepdims=True) * scale
            g_t = tile(g_col[t * H_FOX:(t + 1) * H_FOX, :])
            lgs.append(jnp.where(tok >= t, sn + g_col - g_t, -jnp.inf))
        m0 = m_scr[...]
        m1 = m0
        for lg in lgs:
            m1 = jnp.maximum(m1, lg)
        a1 = jnp.exp(m0 - m1)
        l1 = a1 * l_scr[...]
        acc1 = a1 * acc_scr[...]
        for t, lg in enumerate(lgs):
            pt = jnp.exp(lg - m1)
            l1 = l1 + pt
            acc1 = acc1 + pt * tile(_heads_to_rows(vn[t:t + 1, :]))
        o = acc1 / l1
        outs = [jnp.concatenate([o[t * H_FOX + h:t * H_FOX + h + 1, :] for h in range(H_FOX)], axis=1)
                for t in range(n_new)]
        outs.append(jnp.zeros((o_ref.shape[1] - n_new, W_FOX), F32))
        o_ref[0] = (jnp.concatenate(outs, axis=0) * sg_ref[0].astype(F32)).astype(BF16)


def _fox_sample(z3, lf_new, cache_k, cache_v, bias_rows, page_table, *, n_new, pps):
    b, tp, _ = z3.shape
    n_pages = page_table.shape[1]
    steps = n_pages // pps
    n_rows = n_new * H_FOX

    def page_map(r):
        return lambda i, p, pt: (pt[i * n_pages + (steps - 1 - p) * pps + r], 0, 0, 0)

    kv_specs = [pl.BlockSpec((1, PAGE, H_FOX, HD_FOX), page_map(r)) for r in range(pps)]
    zspec = lambda sec: pl.BlockSpec((1, tp, 1024), lambda i, p, pt: (i, 0, sec))
    kern = functools.partial(_fox_sample_kernel, n_new=n_new, scale=HD_FOX ** -0.5, pps=pps)
    grid_spec = pltpu.PrefetchScalarGridSpec(
        num_scalar_prefetch=1,
        grid=(b, steps),
        in_specs=kv_specs + kv_specs + [
            pl.BlockSpec((1, pps, 1, PAGE * H_FOX), lambda i, p, pt: (i, steps - 1 - p, 0, 0)),
            zspec(ZB_QF), zspec(ZB_KF), zspec(ZB_VF), zspec(ZB_GF),
            pl.BlockSpec((1, tp, H_FOX), lambda i, p, pt: (i, 0, 0)),
        ],
        out_specs=pl.BlockSpec((1, tp, W_FOX), lambda i, p, pt: (i, 0, 0)),
        scratch_shapes=[
            pltpu.VMEM((n_rows, HD_FOX), BF16),
            pltpu.VMEM((n_rows, 1), F32),
            pltpu.VMEM((n_rows, 1), F32),
            pltpu.VMEM((n_rows, HD_FOX), F32),
            pltpu.VMEM((n_rows, 1), F32),
        ],
    )
    return pl.pallas_call(
        kern,
        grid_spec=grid_spec,
        out_shape=jax.ShapeDtypeStruct((b, tp, W_FOX), BF16),
        compiler_params=_params(("arbitrary", "arbitrary")),
        name="fox_sample",
    )(page_table.reshape(-1), *([cache_k] * pps), *([cache_v] * pps), bias_rows, z3, z3, z3, z3, lf_new)


def _merge_kernel(ar_ref, af_ref, mr_ref, mf_ref, x_ref, gate_ref, wb_ref, wo_ref, gf_ref, y_ref):
    p_r = jnp.dot(ar_ref[...], wb_ref[0:W_RET, :], preferred_element_type=F32)
    p_f = jnp.dot(af_ref[...], wb_ref[W_RET:W_RET + W_FOX, :], preferred_element_type=F32)
    merged = mr_ref[...].astype(F32) * p_r + mf_ref[...].astype(F32) * p_f
    out = x_ref[...] + gate_ref[0] * jnp.dot(merged.astype(BF16), wo_ref[...], preferred_element_type=F32)
    y = out * lax.rsqrt(jnp.mean(out * out, axis=-1, keepdims=True) + EPS)
    y_ref[...] = y * gf_ref[...]


def _merge(a_r, a_f, z, x2d, gate, w_branch, w_out, g_final, *, tm):
    rows, d = x2d.shape
    nrb = rows // tm
    groups, r, _ = gate.shape
    bpg = nrb // groups
    return pl.pallas_call(
        _merge_kernel,
        grid=(nrb,),
        in_specs=[
            pl.BlockSpec((tm, W_RET), lambda i: (i, 0)),
            pl.BlockSpec((tm, W_FOX), lambda i: (i, 0)),
            pl.BlockSpec((tm, d), lambda i: (i, 4)),
            pl.BlockSpec((tm, d), lambda i: (i, 5)),
            pl.BlockSpec((tm, d), lambda i: (i, 0)),
            pl.BlockSpec((1, r, d), lambda i: (i // bpg, 0, 0)),
            pl.BlockSpec(w_branch.shape, lambda i: (0, 0)),
            pl.BlockSpec(w_out.shape, lambda i: (0, 0)),
            pl.BlockSpec((1, d), lambda i: (0, 0)),
        ],
        out_specs=pl.BlockSpec((tm, d), lambda i: (i, 0)),
        out_shape=jax.ShapeDtypeStruct((rows, d), F32),
        compiler_params=_params(("arbitrary",)),
        name="merge",
    )(a_r, a_f, z, z, x2d, gate, w_branch, w_out, g_final.reshape(1, d))


def _rope_tables(pos):
    inv = 1.0 / (ROPE_BASE ** (jnp.arange(0, DK_RET, 2, dtype=F32) / DK_RET))
    ang = pos.astype(F32)[:, None] * inv[None, :]
    return jnp.cos(ang), jnp.sin(ang)


def kernel(x_prompt, x_sample, c_prompt, c_sample, cache_k, cache_v, cache_logf, state_ret, page_table,
           w_in, b_fgt, g_norm, w_ada, b_ada, w_branch, w_out, g_final):
    depth = w_in.shape[0]
    assert depth == 1, "the layer loop is written for the single-layer configuration"
    batch, seq, d = x_prompt.shape
    dec_batch, dec_seq, _ = x_sample.shape
    n_pages = page_table.shape[1]
    past_len = n_pages * PAGE
    l = 0

    w_l = w_in[l]
    w_main = jnp.concatenate([w_l[:, :F_COL0], w_l[:, F_COL0 + H_FOX:]], axis=1).astype(BF16)
    w_main = w_main.reshape(d, Z_COLS // PROJ_TN, PROJ_TN).transpose(1, 0, 2)
    w_f = jnp.pad(w_l[:, F_COL0:F_COL0 + H_FOX], ((0, 0), (0, 128 - H_FOX))).astype(BF16)
    wb = w_branch[l].astype(BF16)
    wo = w_out[l].astype(BF16)

    mod = _mod(jnp.concatenate([c_prompt, c_sample], axis=0), w_ada[l], b_ada[l])
    shift, scale, gate = jnp.split(mod, 3, axis=-1)

    rows_p = batch * seq
    xp2 = x_prompt.reshape(rows_p, d)
    cos_p, sin_p = _rope_tables(jnp.arange(seq))
    tm_p = min(1024, seq)
    z_p, kf_p, vf_p, lf_p = _proj(xp2, scale[:batch, None, :], shift[:batch, None, :], g_norm[l],
                                  w_main, w_f, b_fgt[l], cos_p, sin_p, tm=tm_p)
    t_fox = min(512, seq)
    f_t = _cumsum_t(lf_p.reshape(batch, seq, H_FOX).transpose(0, 2, 1), chunk=min(256, seq))
    a_f_p = _fox_prompt(z_p, f_t.reshape(batch * H_FOX, seq // t_fox, 1, t_fox),
                        batch=batch, seq=seq, t=t_fox)
    a_r_p, st_p = _ret_prompt(z_p, batch=batch, seq=seq)
    y_p = _merge(a_r_p, a_f_p, z_p, xp2, gate[:batch, None, :], wb, wo, g_final, tm=min(256, rows_p))

    rows_s = dec_batch * T_PAD
    xs2 = jnp.pad(x_sample, ((0, 0), (0, T_PAD - dec_seq), (0, 0))).reshape(rows_s, d)
    pos_s = past_len + (jnp.arange(rows_s) % T_PAD)
    cos_s, sin_s = _rope_tables(pos_s)
    tm_s = min(256, rows_s)
    rep = lambda a: jnp.repeat(a[batch:], T_PAD, axis=0).reshape(rows_s // tm_s, tm_s, d)
    z_s, kf_s, vf_s, lf_s = _proj(xs2, rep(scale), rep(shift), g_norm[l], w_main, w_f, b_fgt[l],
                                  cos_s, sin_s, tm=tm_s)
    z_s3 = z_s.reshape(dec_batch, T_PAD, Z_COLS)
    a_r_s, st_s = _ret_sample(z_s3, state_ret[l].astype(F32), valid=dec_seq)
    n_pool = cache_k.shape[1]
    pps = min(PAGES_PER_STEP, n_pages)
    bias = _fox_bias(cache_logf[l].reshape(n_pool, H_FOX, PAGE), page_table,
                     pps=min(BIAS_PAGES_PER_STEP, n_pages))
    a_f_s = _fox_sample(z_s3, lf_s.reshape(dec_batch, T_PAD, H_FOX), cache_k[l], cache_v[l],
                        bias.reshape(dec_batch, n_pages, 1, PAGE * H_FOX), page_table, n_new=dec_seq, pps=pps)
    y_s = _merge(a_r_s.reshape(rows_s, W_RET), a_f_s.reshape(rows_s, W_FOX), z_s, xs2, rep(gate),
                 wb, wo, g_final, tm=tm_s)

    take = lambda a, tail: a.reshape((dec_batch, T_PAD) + tail)[:, :dec_seq]
    return (y_p.reshape(batch, seq, d),
            take(y_s, (d,)),
            kf_p.reshape(1, batch, seq, H_FOX, HD_FOX),
            vf_p.reshape(1, batch, seq, H_FOX, HD_FOX),
            lf_p.reshape(1, batch, seq, H_FOX),
            st_p[None],
            take(kf_s, (H_FOX, HD_FOX))[None],
            take(vf_s, (H_FOX, HD_FOX))[None],
            take(lf_s, (H_FOX,))[None],
            st_s[None])
```

```python
import functools

import jax
import jax.numpy as jnp
import numpy as np
from jax import lax
from jax.experimental import pallas as pl
from jax.experimental.pallas import tpu as pltpu

F32 = jnp.float32
BF16 = jnp.bfloat16

D_MODEL = 2048
H_RET, DK_RET, DV_RET = 4, 256, 256
H_FOX, HD_FOX = 8, 128
W_RET = H_RET * DV_RET
W_FOX = H_FOX * HD_FOX
RET_CHUNK = 512
PAGE = 128
ROPE_BASE = 10000.0
EPS = 1e-6
LOG2E = 1.4426950408889634
T_PAD = 16
PAGES_PER_STEP = 8
BIAS_PAGES_PER_STEP = 32
PROJ_ROW_CHUNK = 256
PROJ_WN = 256
PROJ_W_STREAMS = 2
FOX_DIAG_BANDS = 1
Z_COLS = 12288
F_COL0 = 8192

ZB_QR, ZB_KR, ZB_VR, ZB_GR, ZB_QF, ZB_KF, ZB_VF, ZB_GF = range(8)
VMEM_LIMIT = 56 * 1024 * 1024

NT_DIMS = (((1,), (1,)), ((), ()))
TN_DIMS = (((0,), (0,)), ((), ()))


def _params(sem):
    return pltpu.CompilerParams(dimension_semantics=sem, vmem_limit_bytes=VMEM_LIMIT)


def _split3(x):
    hi = x.astype(BF16)
    r1 = x - hi.astype(F32)
    mid = r1.astype(BF16)
    lo = (r1 - mid.astype(F32)).astype(BF16)
    return hi, mid, lo


def _mod_kernel(c_ref, *rest, nw):
    w_refs, b_ref, o_ref = rest[:nw], rest[nw], rest[nw + 1]
    c = c_ref[...]
    sc = c * jax.nn.sigmoid(c)
    wn = w_refs[0].shape[1]
    for p, w_ref in enumerate(w_refs):
        cols = slice(p * wn, (p + 1) * wn)
        o_ref[:, cols] = jnp.dot(sc, w_ref[...], preferred_element_type=F32,
                                 precision=lax.Precision.HIGHEST) + b_ref[:, cols]


def _mod(c, w_ada, b_ada, tn=1024, nw=4):
    n, d = c.shape
    cols = w_ada.shape[1]
    wn = tn // nw
    return pl.pallas_call(
        functools.partial(_mod_kernel, nw=nw),
        grid=(cols // tn,),
        in_specs=[pl.BlockSpec((n, d), lambda j: (0, 0)),
                  *[pl.BlockSpec((d, wn), lambda j, p=p: (0, j * nw + p)) for p in range(nw)],
                  pl.BlockSpec((1, tn), lambda j: (0, j))],
        out_specs=pl.BlockSpec((n, tn), lambda j: (0, j)),
        out_shape=jax.ShapeDtypeStruct((n, cols), F32),
        compiler_params=_params(("arbitrary",)),
        name="mod",
    )(c, *([w_ada] * nw), b_ada.reshape(1, cols))


def _proj_kernel(x_ref, scale_ref, shift_ref, gn_ref, *rest, tn, rc_rows, nw):
    w_refs = rest[:nw]
    wf_ref, bf_ref, cos_ref, sin_ref, z_ref, kf_ref, vf_ref, lf_ref, h_scr = rest[nw:]
    wn = tn // nw
    j = pl.program_id(1)
    per_kb = 1024 // tn

    @pl.when(j == 0)
    def _():
        x = x_ref[...]
        ms = jnp.mean(x * x, axis=-1, keepdims=True)
        xn = x * lax.rsqrt(ms + EPS) * gn_ref[...]
        h = xn * (1.0 + scale_ref[0]) + shift_ref[0]
        hb = h.astype(BF16)
        h_scr[...] = hb
        f = jnp.dot(hb, wf_ref[...], preferred_element_type=F32)
        lf_ref[...] = jax.nn.log_sigmoid(f[:, :H_FOX] + bf_ref[...])

    sec = j // per_kb
    tm = h_scr.shape[0]
    rc = min(rc_rows, tm)

    def run(epilogue):
        for c0 in range(0, tm, rc):
            rows = slice(c0, c0 + rc)
            for p, w_ref in enumerate(w_refs):
                cols = slice(p * wn, (p + 1) * wn)
                epilogue(jnp.dot(h_scr[rows, :], w_ref[0], preferred_element_type=F32), rows, cols)

    def rope_epilogue(acc, rows, cols):
        c = cos_ref[rows, :]
        s = sin_ref[rows, :]
        sc = jnp.where(sec == ZB_KR, DK_RET ** -0.5, 1.0).astype(F32)
        for hh in range(wn // DK_RET):
            lo, mid, hi = hh * DK_RET, hh * DK_RET + DK_RET // 2, (hh + 1) * DK_RET
            x1 = acc[:, lo:mid]
            x2 = acc[:, mid:hi]
            z_ref[rows, cols.start + lo:cols.start + mid] = ((x1 * c - x2 * s) * sc).astype(BF16)
            z_ref[rows, cols.start + mid:cols.start + hi] = ((x1 * s + x2 * c) * sc).astype(BF16)

    def kf_epilogue(acc, rows, cols):
        z_ref[rows, cols] = acc.astype(BF16)
        kf_ref[rows, cols] = acc

    def vf_epilogue(acc, rows, cols):
        z_ref[rows, cols] = acc.astype(BF16)
        vf_ref[rows, cols] = acc

    is_silu = (sec == ZB_GR) | (sec == ZB_GF)
    is_sig = sec > ZB_GF
    ca = jnp.where(is_silu | is_sig, 0.0, 1.0).astype(F32)
    cb = jnp.where(is_silu, 1.0, 0.0).astype(F32)
    cc = jnp.where(is_sig, 1.0, 0.0).astype(F32)

    def gate_epilogue(acc, rows, cols):
        z_ref[rows, cols] = (acc * ca + jax.nn.sigmoid(acc) * (acc * cb + cc)).astype(BF16)

    is_rope = sec <= ZB_KR
    pl.when(is_rope)(lambda: run(rope_epilogue))
    pl.when(sec == ZB_KF)(lambda: run(kf_epilogue))
    pl.when(sec == ZB_VF)(lambda: run(vf_epilogue))
    pl.when(jnp.logical_not(is_rope) & (sec != ZB_KF) & (sec != ZB_VF))(lambda: run(gate_epilogue))


def _proj(x2d, scale, shift, g_norm, w_main, w_f, b_f, cos, sin, *, tm):
    rows, d = x2d.shape
    nw = PROJ_W_STREAMS
    wn = w_main.shape[2]
    tn = nw * wn
    nrb = rows // tm
    ncb = Z_COLS // tn
    per_kb = 1024 // tn
    groups = scale.shape[0]
    bpg = nrb // groups
    r = scale.shape[1]
    ntab = cos.shape[0] // tm
    kern = functools.partial(_proj_kernel, tn=tn, rc_rows=PROJ_ROW_CHUNK, nw=nw)
    kf0, vf0 = ZB_KF * per_kb, ZB_VF * per_kb
    return pl.pallas_call(
        kern,
        grid=(nrb, ncb),
        in_specs=[
            pl.BlockSpec((tm, d), lambda i, j: (i, 0)),
            pl.BlockSpec((1, r, d), lambda i, j: (i // bpg, 0, 0)),
            pl.BlockSpec((1, r, d), lambda i, j: (i // bpg, 0, 0)),
            pl.BlockSpec((1, d), lambda i, j: (0, 0)),
            *[pl.BlockSpec((1, d, wn), lambda i, j, p=p: (j * nw + p, 0, 0)) for p in range(nw)],
            pl.BlockSpec((d, 128), lambda i, j: (0, 0)),
            pl.BlockSpec((1, H_FOX), lambda i, j: (0, 0)),
            pl.BlockSpec((tm, 128), lambda i, j: (i % ntab, 0)),
            pl.BlockSpec((tm, 128), lambda i, j: (i % ntab, 0)),
        ],
        out_specs=[
            pl.BlockSpec((tm, tn), lambda i, j: (i, j)),
            pl.BlockSpec((tm, tn), lambda i, j: (i, jnp.clip(j - kf0, 0, per_kb - 1))),
            pl.BlockSpec((tm, tn), lambda i, j: (i, jnp.clip(j - vf0, 0, per_kb - 1))),
            pl.BlockSpec((tm, H_FOX), lambda i, j: (i, 0)),
        ],
        out_shape=[
            jax.ShapeDtypeStruct((rows, Z_COLS), BF16),
            jax.ShapeDtypeStruct((rows, W_FOX), F32),
            jax.ShapeDtypeStruct((rows, W_FOX), F32),
            jax.ShapeDtypeStruct((rows, H_FOX), F32),
        ],
        scratch_shapes=[pltpu.VMEM((tm, d), BF16)],
        compiler_params=_params(("arbitrary", "arbitrary")),
        name="proj",
    )(x2d, scale, shift, g_norm.reshape(1, d), *([w_main] * nw), w_f, b_f.reshape(1, H_FOX), cos, sin)


def _cumsum_kernel(lt_ref, ft_ref, *, chunk):
    s = lt_ref.shape[2]
    row = lax.broadcasted_iota(jnp.int32, (chunk, chunk), 0)
    col = lax.broadcasted_iota(jnp.int32, (chunk, chunk), 1)
    tri = jnp.where(row <= col, 1.0, 0.0).astype(BF16)
    carry = jnp.zeros((H_FOX, 1), F32)
    for c in range(s // chunk):
        lf = lt_ref[0, :, c * chunk:(c + 1) * chunk]
        terms = [t.astype(F32) for t in _split3(lf)] + [jnp.zeros_like(lf)]
        parts = jnp.dot(jnp.concatenate(terms, axis=0).astype(BF16), tri, preferred_element_type=F32)
        cs = parts[0:H_FOX] + parts[H_FOX:2 * H_FOX] + parts[2 * H_FOX:3 * H_FOX] + carry
        ft_ref[0, :, c * chunk:(c + 1) * chunk] = cs
        carry = cs[:, chunk - 1:chunk]


def _cumsum_t(lf_t, chunk=256):
    b, h, s = lf_t.shape
    return pl.pallas_call(
        functools.partial(_cumsum_kernel, chunk=chunk),
        grid=(b,),
        in_specs=[pl.BlockSpec((1, h, s), lambda i: (i, 0, 0))],
        out_specs=pl.BlockSpec((1, h, s), lambda i: (i, 0, 0)),
        out_shape=jax.ShapeDtypeStruct((b, h, s), F32),
        compiler_params=_params(("arbitrary",)),
        name="cumsum",
    )(lf_t)


def _fox_kernel(q_ref, k_ref, v_ref, sg_ref, ft_ref, o_ref, *, t, scale, ndiag):
    qi = pl.program_id(2)
    c2 = scale * LOG2E

    def update(q, k, v, fk2, state, row0):
        m, l, acc = state
        s = lax.dot_general(q, k, NT_DIMS, preferred_element_type=F32) * c2 - fk2
        if row0 is not None:
            r = lax.broadcasted_iota(jnp.int32, s.shape, 0) + row0
            c = lax.broadcasted_iota(jnp.int32, s.shape, 1)
            s = jnp.where(r >= c, s, -jnp.inf)
        m_new = jnp.maximum(m, jnp.max(s, axis=1, keepdims=True))
        alpha = jnp.exp2(m - m_new)
        p = jnp.exp2(s - m_new)
        l = alpha * l + jnp.sum(p, axis=1, keepdims=True)
        acc = alpha * acc + jnp.dot(p.astype(BF16), v, preferred_element_type=F32)
        return m_new, l, acc

    def full_block(kj, state):
        rows = pl.ds(pl.multiple_of(kj * t, t), t)
        return update(q_ref[...], k_ref[rows, :], v_ref[rows, :], ft_ref[0, kj] * LOG2E, state, None)

    init = (jnp.full((t, 1), -jnp.inf, F32), jnp.zeros((t, 1), F32), jnp.zeros((t, HD_FOX), F32))
    m, l, acc = lax.fori_loop(0, qi, full_block, init)

    rb = t // ndiag
    fk_diag = ft_ref[0, qi] * LOG2E
    for band in range(ndiag):
        rows = slice(band * rb, (band + 1) * rb)
        nk = (band + 1) * rb
        keys = pl.ds(pl.multiple_of(qi * t, t), nk)
        mb, lb, accb = update(q_ref[rows, :], k_ref[keys, :], v_ref[keys, :], fk_diag[:, :nk],
                              (m[rows], l[rows], acc[rows]), band * rb)
        o_ref[rows, :] = ((accb / lb) * sg_ref[rows, :].astype(F32)).astype(BF16)


def _fox_prompt(z, f_t, *, batch, seq, t=512):
    nq = seq // t
    cpb = 1024 // HD_FOX
    kern = functools.partial(_fox_kernel, t=t, scale=HD_FOX ** -0.5, ndiag=min(FOX_DIAG_BANDS, t // 128))
    return pl.pallas_call(
        kern,
        grid=(batch, H_FOX, nq),
        in_specs=[
            pl.BlockSpec((t, HD_FOX), lambda b, h, i: (b * nq + i, ZB_QF * cpb + h)),
            pl.BlockSpec((seq, HD_FOX), lambda b, h, i: (b, ZB_KF * cpb + h)),
            pl.BlockSpec((seq, HD_FOX), lambda b, h, i: (b, ZB_VF * cpb + h)),
            pl.BlockSpec((t, HD_FOX), lambda b, h, i: (b * nq + i, ZB_GF * cpb + h)),
            pl.BlockSpec((1, nq, 1, t), lambda b, h, i: (b * H_FOX + h, 0, 0, 0)),
        ],
        out_specs=pl.BlockSpec((t, HD_FOX), lambda b, h, i: (b * nq + i, h)),
        out_shape=jax.ShapeDtypeStruct((batch * seq, W_FOX), BF16),
        compiler_params=_params(("arbitrary", "arbitrary", "arbitrary")),
        name="fox_prompt",
    )(z, z, z, z, f_t)


def _head_norm_gate(o, sg):
    on = o * lax.rsqrt(jnp.mean(o * o, axis=-1, keepdims=True) + EPS)
    return (on * sg.astype(F32)).astype(BF16)


def _ret_kernel(q_ref, k_ref, v_ref, sg_ref, dm_ref, qd_ref, kd_ref, gl_ref, a_ref, st_ref, s_scr, *, chunk):
    s_scr[...] = jnp.zeros_like(s_scr)
    dm = dm_ref[0]
    qd = qd_ref[0]
    kd = kd_ref[0]
    gl = gl_ref[0]
    nc = q_ref.shape[0] // chunk

    def body(c, carry):
        rows = pl.ds(pl.multiple_of(c * chunk, chunk), chunk)
        q = q_ref[rows, :]
        k = k_ref[rows, :]
        v = v_ref[rows, :]
        st = s_scr[...]
        sc = lax.dot_general(q, k, NT_DIMS, preferred_element_type=F32) * dm
        o = (jnp.dot(sc.astype(BF16), v, preferred_element_type=F32)
             + qd * jnp.dot(q, st.astype(BF16), preferred_element_type=F32))
        kdec = (k.astype(F32) * kd).astype(BF16)
        s_scr[...] = gl * st + lax.dot_general(kdec, v, TN_DIMS, preferred_element_type=F32)
        a_ref[rows, :] = _head_norm_gate(o, sg_ref[rows, :])
        return carry

    lax.fori_loop(0, nc, body, 0)
    st_ref[0, 0] = s_scr[...]


def _ret_tables(length, valid):
    lg = jnp.log(1.0 - 2.0 ** (-5.0 - jnp.arange(H_RET, dtype=F32)))
    idx = jnp.arange(length, dtype=F32)
    diff = idx[:, None] - idx[None, :]
    ok = (diff >= 0) & (idx[None, :] < valid)
    dm = jnp.where(ok[None], jnp.exp(lg[:, None, None] * jnp.where(ok, diff, 0.0)[None]), 0.0)
    qd = jnp.exp(lg[:, None] * (idx[None, :] + 1.0))[..., None]
    kd = jnp.where(idx[None, :] < valid, jnp.exp(lg[:, None] * (valid - 1.0 - idx[None, :])), 0.0)[..., None]
    gl = jnp.exp(lg * valid)[:, None, None]
    return dm, qd, kd, gl


def _ret_prompt(z, *, batch, seq, chunk=RET_CHUNK):
    dm, qd, kd, gl = _ret_tables(chunk, chunk)
    cpb = 1024 // DK_RET
    kern = functools.partial(_ret_kernel, chunk=chunk)
    zspec = lambda sec: pl.BlockSpec((seq, DK_RET), lambda b, h: (b, sec * cpb + h))
    tab = lambda shape: pl.BlockSpec((1,) + shape, lambda b, h: (h, 0, 0))
    return pl.pallas_call(
        kern,
        grid=(batch, H_RET),
        in_specs=[zspec(ZB_QR), zspec(ZB_KR), zspec(ZB_VR), zspec(ZB_GR),
                  tab((chunk, chunk)), tab((chunk, 1)), tab((chunk, 1)), tab((1, 1))],
        out_specs=[pl.BlockSpec((seq, DV_RET), lambda b, h: (b, h)),
                   pl.BlockSpec((1, 1, DK_RET, DV_RET), lambda b, h: (b, h, 0, 0))],
        out_shape=[jax.ShapeDtypeStruct((batch * seq, W_RET), BF16),
                   jax.ShapeDtypeStruct((batch, H_RET, DK_RET, DV_RET), F32)],
        scratch_shapes=[pltpu.VMEM((DK_RET, DV_RET), F32)],
        compiler_params=_params(("arbitrary", "arbitrary")),
        name="ret_prompt",
    )(z, z, z, z, dm, qd, kd, gl)


def _ret_sample_kernel(q_ref, k_ref, v_ref, sg_ref, st_ref, dm_ref, qd_ref, kd_ref, gl_ref, a_ref, nst_ref):
    tp = q_ref.shape[1]
    pad = jnp.zeros((PAGE - tp, DK_RET), BF16)
    for h in range(H_RET):
        cols = slice(h * DK_RET, (h + 1) * DK_RET)
        q = q_ref[0, :, cols]
        k = k_ref[0, :, cols]
        v = v_ref[0, :, cols]
        kp = jnp.concatenate([k, pad], axis=0)
        vp = jnp.concatenate([v, pad], axis=0)
        kdp = jnp.concatenate([(k.astype(F32) * kd_ref[h]).astype(BF16), pad], axis=0)
        st = st_ref[0, h]
        sc = lax.dot_general(q, kp, NT_DIMS, preferred_element_type=F32) * dm_ref[h]
        o = (jnp.dot(sc.astype(BF16), vp, preferred_element_type=F32)
             + qd_ref[h] * jnp.dot(q, st.astype(BF16), preferred_element_type=F32))
        nst_ref[0, h] = gl_ref[h] * st + lax.dot_general(kdp, vp, TN_DIMS, preferred_element_type=F32)
        a_ref[0, :, cols] = _head_norm_gate(o, sg_ref[0, :, cols])


def _ret_sample(z3, state, *, valid):
    b, tp, _ = z3.shape
    dm, qd, kd, gl = _ret_tables(tp, valid)
    dm = jnp.pad(dm, ((0, 0), (0, 0), (0, PAGE - tp)))
    zspec = lambda sec: pl.BlockSpec((1, tp, 1024), lambda i: (i, 0, sec))
    full = lambda a: pl.BlockSpec(a.shape, lambda i: (0,) * a.ndim)
    st_spec = pl.BlockSpec((1, H_RET, DK_RET, DV_RET), lambda i: (i, 0, 0, 0))
    return pl.pallas_call(
        _ret_sample_kernel,
        grid=(b,),
        in_specs=[zspec(ZB_QR), zspec(ZB_KR), zspec(ZB_VR), zspec(ZB_GR), st_spec,
                  full(dm), full(qd), full(kd), full(gl)],
        out_specs=[pl.BlockSpec((1, tp, W_RET), lambda i: (i, 0, 0)), st_spec],
        out_shape=[jax.ShapeDtypeStruct((b, tp, W_RET), BF16),
                   jax.ShapeDtypeStruct(state.shape, F32)],
        compiler_params=_params(("arbitrary",)),
        name="ret_sample",
    )(z3, z3, z3, z3, state, dm, qd, kd, gl)


def _fox_bias_kernel(pt_ref, *refs, pps):
    del pt_ref
    lf_refs = refs[:pps]
    o_ref, carry_scr = refs[pps], refs[pps + 1]

    @pl.when(pl.program_id(1) == 0)
    def _():
        carry_scr[...] = jnp.zeros_like(carry_scr)

    lf = jnp.concatenate([r[0] for r in lf_refs], axis=0)
    nl = lf.shape[0]
    jj = lax.broadcasted_iota(jnp.int32, (PAGE, 2 * PAGE), 0)
    ss = lax.broadcasted_iota(jnp.int32, (PAGE, 2 * PAGE), 1)
    same_head = (jj % H_FOX) == (ss % H_FOX)
    w = jnp.where(same_head & ((ss >= PAGE) | (jj // H_FOX > ss // H_FOX)), 1.0, 0.0).astype(BF16)
    parts = jnp.dot(jnp.concatenate(_split3(lf), axis=0), w, preferred_element_type=F32)
    both = parts[0:nl] + parts[nl:2 * nl] + parts[2 * nl:3 * nl]
    within = both[:, :PAGE]
    tot = both[:, PAGE:]
    kdim = max(nl, PAGE)
    ri = lax.broadcasted_iota(jnp.int32, (nl, kdim), 0)
    ci = lax.broadcasted_iota(jnp.int32, (nl, kdim), 1)
    later = jnp.where((ci > ri) & (ci < nl), 1.0, 0.0).astype(BF16)
    t3 = jnp.concatenate(_split3(tot), axis=1)
    if nl < kdim:
        t3 = jnp.concatenate([t3, jnp.zeros((kdim - nl, 3 * PAGE), BF16)], axis=0)
    cross3 = jnp.dot(later, t3, preferred_element_type=F32)
    cross = cross3[:, :PAGE] + cross3[:, PAGE:2 * PAGE] + cross3[:, 2 * PAGE:]
    carry = carry_scr[...]
    r = within + cross + carry
    carry_scr[...] = carry + jnp.sum(tot, axis=0, keepdims=True)
    rows_per_page = PAGE * H_FOX // PAGE
    for pg in range(pps):
        o_ref[0, pg] = r[pg * rows_per_page:(pg + 1) * rows_per_page, :]


def _fox_bias(cache_lf, page_table, *, pps):
    b, n_pages = page_table.shape
    steps = n_pages // pps
    rpp = cache_lf.shape[1]

    def page_map(r):
        return lambda i, p, pt: (pt[i * n_pages + (steps - 1 - p) * pps + r], 0, 0)

    grid_spec = pltpu.PrefetchScalarGridSpec(
        num_scalar_prefetch=1,
        grid=(b, steps),
        in_specs=[pl.BlockSpec((1, rpp, PAGE), page_map(r)) for r in range(pps)],
        out_specs=pl.BlockSpec((1, pps, rpp, PAGE), lambda i, p, pt: (i, steps - 1 - p, 0, 0)),
        scratch_shapes=[pltpu.VMEM((1, PAGE), F32)],
    )
    return pl.pallas_call(
        functools.partial(_fox_bias_kernel, pps=pps),
        grid_spec=grid_spec,
        out_shape=jax.ShapeDtypeStruct((b, n_pages, rpp, PAGE), F32),
        compiler_params=_params(("arbitrary", "arbitrary")),
        name="fox_bias",
    )(page_table.reshape(-1), *([cache_lf] * pps))


def _heads_to_rows(x_row):
    return jnp.concatenate([x_row[:, h * HD_FOX:(h + 1) * HD_FOX] for h in range(H_FOX)], axis=0)


def _fox_sample_kernel(pt_ref, *refs, n_new, scale, pps):
    del pt_ref
    k_refs = refs[0:pps]
    v_refs = refs[pps:2 * pps]
    r_ref, q_ref, kn_ref, vn_ref, sg_ref, lfn_ref, o_ref = refs[2 * pps:2 * pps + 7]
    q_scr, m_scr, l_scr, acc_scr, g_scr = refs[2 * pps + 7:]
    p_id = pl.program_id(1)
    n_rows = n_new * H_FOX
    n_keys = PAGE * H_FOX

    @pl.when(p_id == 0)
    def _():
        q = q_ref[0].astype(F32)
        q_scr[...] = jnp.concatenate([_heads_to_rows(q[t:t + 1, :]) for t in range(n_new)], axis=0).astype(BF16)
        m_scr[...] = jnp.full_like(m_scr, -jnp.inf)
        l_scr[...] = jnp.zeros_like(l_scr)
        acc_scr[...] = jnp.zeros_like(acc_scr)
        lfn = lfn_ref[0]
        r8 = lax.broadcasted_iota(jnp.int32, (H_FOX, H_FOX), 0)
        c8 = lax.broadcasted_iota(jnp.int32, (H_FOX, H_FOX), 1)
        g = jnp.zeros((1, H_FOX), F32)
        cols = []
        for t in range(n_new):
            g = g + lfn[t:t + 1, :]
            cols.append(jnp.sum(jnp.where(r8 == c8, jnp.broadcast_to(g, (H_FOX, H_FOX)), 0.0), axis=1, keepdims=True))
        g_scr[...] = jnp.concatenate(cols, axis=0)

    qm = q_scr[...]
    gcol = g_scr[...]
    row = lax.broadcasted_iota(jnp.int32, (n_rows, n_keys), 0)
    lane = lax.broadcasted_iota(jnp.int32, (n_rows, n_keys), 1)
    own_head = (lane % H_FOX) == (row % H_FOX)

    logits = []
    for r in range(pps):
        kp = k_refs[r][0].reshape(n_keys, HD_FOX).astype(BF16)
        s = lax.dot_general(qm, kp, NT_DIMS, preferred_element_type=F32)
        logits.append(jnp.where(own_head, s * scale + gcol + r_ref[0, r], -jnp.inf))
    m_old = m_scr[...]
    m_new = m_old
    for lg in logits:
        m_new = jnp.maximum(m_new, jnp.max(lg, axis=1, keepdims=True))
    alpha = jnp.exp(m_old - m_new)
    l = alpha * l_scr[...]
    acc = alpha * acc_scr[...]
    for r, lg in enumerate(logits):
        p = jnp.exp(lg - m_new)
        l = l + jnp.sum(p, axis=1, keepdims=True)
        vp = v_refs[r][0].reshape(n_keys, HD_FOX).astype(BF16)
        acc = acc + jnp.dot(p.astype(BF16), vp, preferred_element_type=F32)
    m_scr[...] = m_new
    l_scr[...] = l
    acc_scr[...] = acc

    @pl.when(p_id == pl.num_programs(1) - 1)
    def _():
        qf = q_scr[...].astype(F32)
        kn = kn_ref[0].astype(F32)
        vn = vn_ref[0].astype(F32)
        g_col = g_scr[...]
        tok = lax.broadcasted_iota(jnp.int32, (n_rows, 1), 0) // H_FOX
        tile = lambda x: jnp.concatenate([x] * n_new, axis=0)
        lgs = []
        for t in range(n_new):
            sn = jnp.sum(qf * tile(_heads_to_rows(kn[t:t + 1, :])), axis=1, keepdims=True) * scale
            g_t = tile(g_col[t * H_FOX:(t + 1) * H_FOX, :])
            lgs.append(jnp.where(tok >= t, sn + g_col - g_t, -jnp.inf))
        m0 = m_scr[...]
        m1 = m0
        for lg in lgs:
            m1 = jnp.maximum(m1, lg)
        a1 = jnp.exp(m0 - m1)
        l1 = a1 * l_scr[...]
        acc1 = a1 * acc_scr[...]
        for t, lg in enumerate(lgs):
            pt = jnp.exp(lg - m1)
            l1 = l1 + pt
            acc1 = acc1 + pt * tile(_heads_to_rows(vn[t:t + 1, :]))
        o = acc1 / l1
        outs = [jnp.concatenate([o[t * H_FOX + h:t * H_FOX + h + 1, :] for h in range(H_FOX)], axis=1)
                for t in range(n_new)]
        outs.append(jnp.zeros((o_ref.shape[1] - n_new, W_FOX), F32))
        o_ref[0] = (jnp.concatenate(outs, axis=0) * sg_ref[0].astype(F32)).astype(BF16)


def _fox_sample(z3, lf_new, cache_k, cache_v, bias_rows, page_table, *, n_new, pps):
    b, tp, _ = z3.shape
    n_pages = page_table.shape[1]
    steps = n_pages // pps
    n_rows = n_new * H_FOX

    def page_map(r):
        return lambda i, p, pt: (pt[i * n_pages + (steps - 1 - p) * pps + r], 0, 0, 0)

    kv_specs = [pl.BlockSpec((1, PAGE, H_FOX, HD_FOX), page_map(r)) for r in range(pps)]
    zspec = lambda sec: pl.BlockSpec((1, tp, 1024), lambda i, p, pt: (i, 0, sec))
    kern = functools.partial(_fox_sample_kernel, n_new=n_new, scale=HD_FOX ** -0.5, pps=pps)
    grid_spec = pltpu.PrefetchScalarGridSpec(
        num_scalar_prefetch=1,
        grid=(b, steps),
        in_specs=kv_specs + kv_specs + [
            pl.BlockSpec((1, pps, 1, PAGE * H_FOX), lambda i, p, pt: (i, steps - 1 - p, 0, 0)),
            zspec(ZB_QF), zspec(ZB_KF), zspec(ZB_VF), zspec(ZB_GF),
            pl.BlockSpec((1, tp, H_FOX), lambda i, p, pt: (i, 0, 0)),
        ],
        out_specs=pl.BlockSpec((1, tp, W_FOX), lambda i, p, pt: (i, 0, 0)),
        scratch_shapes=[
            pltpu.VMEM((n_rows, HD_FOX), BF16),
            pltpu.VMEM((n_rows, 1), F32),
            pltpu.VMEM((n_rows, 1), F32),
            pltpu.VMEM((n_rows, HD_FOX), F32),
            pltpu.VMEM((n_rows, 1), F32),
        ],
    )
    return pl.pallas_call(
        kern,
        grid_spec=grid_spec,
        out_shape=jax.ShapeDtypeStruct((b, tp, W_FOX), BF16),
        compiler_params=_params(("arbitrary", "arbitrary")),
        name="fox_sample",
    )(page_table.reshape(-1), *([cache_k] * pps), *([cache_v] * pps), bias_rows, z3, z3, z3, z3, lf_new)


def _merge_kernel(ar_ref, af_ref, mr_ref, mf_ref, x_ref, gate_ref, wb_ref, wo_ref, gf_ref, y_ref):
    p_r = jnp.dot(ar_ref[...], wb_ref[0:W_RET, :], preferred_element_type=F32)
    p_f = jnp.dot(af_ref[...], wb_ref[W_RET:W_RET + W_FOX, :], preferred_element_type=F32)
    merged = mr_ref[...].astype(F32) * p_r + mf_ref[...].astype(F32) * p_f
    out = x_ref[...] + gate_ref[0] * jnp.dot(merged.astype(BF16), wo_ref[...], preferred_element_type=F32)
    y = out * lax.rsqrt(jnp.mean(out * out, axis=-1, keepdims=True) + EPS)
    y_ref[...] = y * gf_ref[...]


def _merge(a_r, a_f, z, x2d, gate, w_branch, w_out, g_final, *, tm):
    rows, d = x2d.shape
    nrb = rows // tm
    groups, r, _ = gate.shape
    bpg = nrb // groups
    return pl.pallas_call(
        _merge_kernel,
        grid=(nrb,),
        in_specs=[
            pl.BlockSpec((tm, W_RET), lambda i: (i, 0)),
            pl.BlockSpec((tm, W_FOX), lambda i: (i, 0)),
            pl.BlockSpec((tm, d), lambda i: (i, 4)),
            pl.BlockSpec((tm, d), lambda i: (i, 5)),
            pl.BlockSpec((tm, d), lambda i: (i, 0)),
            pl.BlockSpec((1, r, d), lambda i: (i // bpg, 0, 0)),
            pl.BlockSpec(w_branch.shape, lambda i: (0, 0)),
            pl.BlockSpec(w_out.shape, lambda i: (0, 0)),
            pl.BlockSpec((1, d), lambda i: (0, 0)),
        ],
        out_specs=pl.BlockSpec((tm, d), lambda i: (i, 0)),
        out_shape=jax.ShapeDtypeStruct((rows, d), F32),
        compiler_params=_params(("arbitrary",)),
        name="merge",
    )(a_r, a_f, z, z, x2d, gate, w_branch, w_out, g_final.reshape(1, d))


def _rope_tables(pos):
    inv = 1.0 / (ROPE_BASE ** (jnp.arange(0, DK_RET, 2, dtype=F32) / DK_RET))
    ang = pos.astype(F32)[:, None] * inv[None, :]
    return jnp.cos(ang), jnp.sin(ang)


def kernel(x_prompt, x_sample, c_prompt, c_sample, cache_k, cache_v, cache_logf, state_ret, page_table,
           w_in, b_fgt, g_norm, w_ada, b_ada, w_branch, w_out, g_final):
    depth = w_in.shape[0]
    assert depth == 1, "the layer loop is written for the single-layer configuration"
    batch, seq, d = x_prompt.shape
    dec_batch, dec_seq, _ = x_sample.shape
    n_pages = page_table.shape[1]
    past_len = n_pages * PAGE
    l = 0

    w_l = w_in[l]
    w_main = jnp.concatenate([w_l[:, :F_COL0], w_l[:, F_COL0 + H_FOX:]], axis=1).astype(BF16)
    w_main = w_main.reshape(d, Z_COLS // PROJ_WN, PROJ_WN).transpose(1, 0, 2)
    w_f = jnp.pad(w_l[:, F_COL0:F_COL0 + H_FOX], ((0, 0), (0, 128 - H_FOX))).astype(BF16)
    wb = w_branch[l].astype(BF16)
    wo = w_out[l].astype(BF16)

    mod = _mod(jnp.concatenate([c_prompt, c_sample], axis=0), w_ada[l], b_ada[l])
    shift, scale, gate = jnp.split(mod, 3, axis=-1)

    rows_p = batch * seq
    xp2 = x_prompt.reshape(rows_p, d)
    cos_p, sin_p = _rope_tables(jnp.arange(seq))
    tm_p = min(1024, seq)
    z_p, kf_p, vf_p, lf_p = _proj(xp2, scale[:batch, None, :], shift[:batch, None, :], g_norm[l],
                                  w_main, w_f, b_fgt[l], cos_p, sin_p, tm=tm_p)
    t_fox = min(512, seq)
    f_t = _cumsum_t(lf_p.reshape(batch, seq, H_FOX).transpose(0, 2, 1), chunk=min(256, seq))
    a_f_p = _fox_prompt(z_p, f_t.reshape(batch * H_FOX, seq // t_fox, 1, t_fox),
                        batch=batch, seq=seq, t=t_fox)
    a_r_p, st_p = _ret_prompt(z_p, batch=batch, seq=seq)
    y_p = _merge(a_r_p, a_f_p, z_p, xp2, gate[:batch, None, :], wb, wo, g_final, tm=min(256, rows_p))

    rows_s = dec_batch * T_PAD
    xs2 = jnp.pad(x_sample, ((0, 0), (0, T_PAD - dec_seq), (0, 0))).reshape(rows_s, d)
    pos_s = past_len + (jnp.arange(rows_s) % T_PAD)
    cos_s, sin_s = _rope_tables(pos_s)
    tm_s = min(256, rows_s)
    rep = lambda a: jnp.repeat(a[batch:], T_PAD, axis=0).reshape(rows_s // tm_s, tm_s, d)
    z_s, kf_s, vf_s, lf_s = _proj(xs2, rep(scale), rep(shift), g_norm[l], w_main, w_f, b_fgt[l],
                                  cos_s, sin_s, tm=tm_s)
    z_s3 = z_s.reshape(dec_batch, T_PAD, Z_COLS)
    a_r_s, st_s = _ret_sample(z_s3, state_ret[l].astype(F32), valid=dec_seq)
    n_pool = cache_k.shape[1]
    pps = min(PAGES_PER_STEP, n_pages)
    bias = _fox_bias(cache_logf[l].reshape(n_pool, H_FOX, PAGE), page_table,
                     pps=min(BIAS_PAGES_PER_STEP, n_pages))
    a_f_s = _fox_sample(z_s3, lf_s.reshape(dec_batch, T_PAD, H_FOX), cache_k[l], cache_v[l],
                        bias.reshape(dec_batch, n_pages, 1, PAGE * H_FOX), page_table, n_new=dec_seq, pps=pps)
    y_s = _merge(a_r_s.reshape(rows_s, W_RET), a_f_s.reshape(rows_s, W_FOX), z_s, xs2, rep(gate),
                 wb, wo, g_final, tm=tm_s)

    take = lambda a, tail: a.reshape((dec_batch, T_PAD) + tail)[:, :dec_seq]
    return (y_p.reshape(batch, seq, d),
            take(y_s, (d,)),
            kf_p.reshape(1, batch, seq, H_FOX, HD_FOX),
            vf_p.reshape(1, batch, seq, H_FOX, HD_FOX),
            lf_p.reshape(1, batch, seq, H_FOX),
            st_p[None],
            take(kf_s, (H_FOX, HD_FOX))[None],
            take(vf_s, (H_FOX, HD_FOX))[None],
            take(lf_s, (H_FOX,))[None],
            st_s[None])
```

```python
import functools

import jax
import jax.numpy as jnp
import numpy as np
from jax import lax
from jax.experimental import pallas as pl
from jax.experimental.pallas import tpu as pltpu

F32 = jnp.float32
BF16 = jnp.bfloat16

D_MODEL = 2048
H_RET, DK_RET, DV_RET = 4, 256, 256
H_FOX, HD_FOX = 8, 128
W_RET = H_RET * DV_RET
W_FOX = H_FOX * HD_FOX
RET_CHUNK = 512
PAGE = 128
ROPE_BASE = 10000.0
EPS = 1e-6
LOG2E = 1.4426950408889634
T_PAD = 16
PAGES_PER_STEP = 8
BIAS_PAGES_PER_STEP = 32
PROJ_ROW_CHUNK = 512
PROJ_WN = 256
PROJ_W_STREAMS = 2
FOX_DIAG_BANDS = 1
Z_COLS = 12288
F_COL0 = 8192

ZB_QR, ZB_KR, ZB_VR, ZB_GR, ZB_QF, ZB_KF, ZB_VF, ZB_GF = range(8)
VMEM_LIMIT = 56 * 1024 * 1024

NT_DIMS = (((1,), (1,)), ((), ()))
TN_DIMS = (((0,), (0,)), ((), ()))


def _params(sem):
    return pltpu.CompilerParams(dimension_semantics=sem, vmem_limit_bytes=VMEM_LIMIT)


def _split3(x):
    hi = x.astype(BF16)
    r1 = x - hi.astype(F32)
    mid = r1.astype(BF16)
    lo = (r1 - mid.astype(F32)).astype(BF16)
    return hi, mid, lo


def _mod_kernel(c_ref, *rest, nw):
    w_refs, b_ref, o_ref = rest[:nw], rest[nw], rest[nw + 1]
    c = c_ref[...]
    sc = c * jax.nn.sigmoid(c)
    wn = w_refs[0].shape[1]
    for p, w_ref in enumerate(w_refs):
        cols = slice(p * wn, (p + 1) * wn)
        o_ref[:, cols] = jnp.dot(sc, w_ref[...], preferred_element_type=F32,
                                 precision=lax.Precision.HIGHEST) + b_ref[:, cols]


def _mod(c, w_ada, b_ada, tn=1024, nw=4):
    n, d = c.shape
    cols = w_ada.shape[1]
    wn = tn // nw
    return pl.pallas_call(
        functools.partial(_mod_kernel, nw=nw),
        grid=(cols // tn,),
        in_specs=[pl.BlockSpec((n, d), lambda j: (0, 0)),
                  *[pl.BlockSpec((d, wn), lambda j, p=p: (0, j * nw + p)) for p in range(nw)],
                  pl.BlockSpec((1, tn), lambda j: (0, j))],
        out_specs=pl.BlockSpec((n, tn), lambda j: (0, j)),
        out_shape=jax.ShapeDtypeStruct((n, cols), F32),
        compiler_params=_params(("arbitrary",)),
        name="mod",
    )(c, *([w_ada] * nw), b_ada.reshape(1, cols))


def _norm_kernel(x_ref, scale_ref, shift_ref, gn_ref, wf_ref, bf_ref, h_ref, lf_ref):
    x = x_ref[...]
    ms = jnp.mean(x * x, axis=-1, keepdims=True)
    xn = x * lax.rsqrt(ms + EPS) * gn_ref[...]
    h = xn * (1.0 + scale_ref[0]) + shift_ref[0]
    hb = h.astype(BF16)
    h_ref[...] = hb
    f = jnp.dot(hb, wf_ref[...], preferred_element_type=F32)
    lf_ref[...] = jax.nn.log_sigmoid(f[:, :H_FOX] + bf_ref[...])


def _norm(x2d, scale, shift, g_norm, w_f, b_f, *, tm):
    rows, d = x2d.shape
    nrb = rows // tm
    groups, r, _ = scale.shape
    bpg = nrb // groups
    return pl.pallas_call(
        _norm_kernel,
        grid=(nrb,),
        in_specs=[
            pl.BlockSpec((tm, d), lambda i: (i, 0)),
            pl.BlockSpec((1, r, d), lambda i: (i // bpg, 0, 0)),
            pl.BlockSpec((1, r, d), lambda i: (i // bpg, 0, 0)),
            pl.BlockSpec((1, d), lambda i: (0, 0)),
            pl.BlockSpec((d, 128), lambda i: (0, 0)),
            pl.BlockSpec((1, H_FOX), lambda i: (0, 0)),
        ],
        out_specs=[pl.BlockSpec((tm, d), lambda i: (i, 0)),
                   pl.BlockSpec((tm, H_FOX), lambda i: (i, 0))],
        out_shape=[jax.ShapeDtypeStruct((rows, d), BF16),
                   jax.ShapeDtypeStruct((rows, H_FOX), F32)],
        compiler_params=_params(("arbitrary",)),
        name="norm",
    )(x2d, scale, shift, g_norm.reshape(1, d), w_f, b_f.reshape(1, H_FOX))


def _proj_kernel(h_ref, *rest, tn, rc_rows, nw, n_lo_blocks):
    lo_refs, hi_refs = rest[:nw], rest[nw:2 * nw]
    cos_ref, sin_ref, z_ref, kf_ref, vf_ref = rest[2 * nw:]
    wn = tn // nw
    j = pl.program_id(1)
    per_kb = 1024 // tn
    sec = j // per_kb
    tm = h_ref.shape[0]
    rc = min(rc_rows, tm)

    def run(epilogue, w_refs):
        for c0 in range(0, tm, rc):
            rows = slice(c0, c0 + rc)
            for p, w_ref in enumerate(w_refs):
                cols = slice(p * wn, (p + 1) * wn)
                epilogue(jnp.dot(h_ref[rows, :], w_ref[0], preferred_element_type=F32), rows, cols)

    def rope_epilogue(acc, rows, cols):
        c = cos_ref[rows, :]
        s = sin_ref[rows, :]
        sc = jnp.where(sec == ZB_KR, DK_RET ** -0.5, 1.0).astype(F32)
        for hh in range(wn // DK_RET):
            lo, mid, hi = hh * DK_RET, hh * DK_RET + DK_RET // 2, (hh + 1) * DK_RET
            x1 = acc[:, lo:mid]
            x2 = acc[:, mid:hi]
            z_ref[rows, cols.start + lo:cols.start + mid] = ((x1 * c - x2 * s) * sc).astype(BF16)
            z_ref[rows, cols.start + mid:cols.start + hi] = ((x1 * s + x2 * c) * sc).astype(BF16)

    def kf_epilogue(acc, rows, cols):
        z_ref[rows, cols] = acc.astype(BF16)
        kf_ref[rows, cols] = acc

    def vf_epilogue(acc, rows, cols):
        z_ref[rows, cols] = acc.astype(BF16)
        vf_ref[rows, cols] = acc

    is_silu = (sec == ZB_GR) | (sec == ZB_GF)
    ca = jnp.where(is_silu, 0.0, 1.0).astype(F32)
    cb = jnp.where(is_silu, 1.0, 0.0).astype(F32)

    def plain_or_silu_epilogue(acc, rows, cols):
        z_ref[rows, cols] = (acc * ca + jax.nn.sigmoid(acc) * (acc * cb)).astype(BF16)

    def sigmoid_epilogue(acc, rows, cols):
        z_ref[rows, cols] = jax.nn.sigmoid(acc).astype(BF16)

    is_rope = sec <= ZB_KR
    is_hi = j >= n_lo_blocks
    pl.when(is_rope)(lambda: run(rope_epilogue, lo_refs))
    pl.when(sec == ZB_KF)(lambda: run(kf_epilogue, lo_refs))
    pl.when(sec == ZB_VF)(lambda: run(vf_epilogue, lo_refs))
    pl.when(is_hi)(lambda: run(sigmoid_epilogue, hi_refs))
    pl.when(jnp.logical_not(is_rope | is_hi) & (sec != ZB_KF) & (sec != ZB_VF))(
        lambda: run(plain_or_silu_epilogue, lo_refs))


def _proj(h, w_lo, w_hi, cos, sin, *, tm):
    rows, d = h.shape
    nw = PROJ_W_STREAMS
    n_lo, _, wn = w_lo.shape
    n_hi = w_hi.shape[0]
    tn = nw * wn
    nrb = rows // tm
    ncb = Z_COLS // tn
    per_kb = 1024 // tn
    n_lo_blocks = n_lo // nw
    ntab = cos.shape[0] // tm
    kern = functools.partial(_proj_kernel, tn=tn, rc_rows=PROJ_ROW_CHUNK, nw=nw, n_lo_blocks=n_lo_blocks)
    kf0, vf0 = ZB_KF * per_kb, ZB_VF * per_kb
    return pl.pallas_call(
        kern,
        grid=(nrb, ncb),
        in_specs=[
            pl.BlockSpec((tm, d), lambda i, j: (i, 0)),
            *[pl.BlockSpec((1, d, wn), lambda i, j, p=p: (jnp.minimum(j * nw + p, n_lo - 1), 0, 0))
              for p in range(nw)],
            *[pl.BlockSpec((1, d, wn), lambda i, j, p=p: (jnp.clip(j * nw + p - n_lo, 0, n_hi - 1), 0, 0))
              for p in range(nw)],
            pl.BlockSpec((tm, 128), lambda i, j: (i % ntab, 0)),
            pl.BlockSpec((tm, 128), lambda i, j: (i % ntab, 0)),
        ],
        out_specs=[
            pl.BlockSpec((tm, tn), lambda i, j: (i, j)),
            pl.BlockSpec((tm, tn), lambda i, j: (i, jnp.clip(j - kf0, 0, per_kb - 1))),
            pl.BlockSpec((tm, tn), lambda i, j: (i, jnp.clip(j - vf0, 0, per_kb - 1))),
        ],
        out_shape=[
            jax.ShapeDtypeStruct((rows, Z_COLS), BF16),
            jax.ShapeDtypeStruct((rows, W_FOX), F32),
            jax.ShapeDtypeStruct((rows, W_FOX), F32),
        ],
        compiler_params=_params(("arbitrary", "arbitrary")),
        name="proj",
    )(h, *([w_lo] * nw), *([w_hi] * nw), cos, sin)


def _cumsum_kernel(lt_ref, ft_ref, *, chunk):
    s = lt_ref.shape[2]
    row = lax.broadcasted_iota(jnp.int32, (chunk, chunk), 0)
    col = lax.broadcasted_iota(jnp.int32, (chunk, chunk), 1)
    tri = jnp.where(row <= col, 1.0, 0.0).astype(BF16)
    carry = jnp.zeros((H_FOX, 1), F32)
    for c in range(s // chunk):
        lf = lt_ref[0, :, c * chunk:(c + 1) * chunk]
        terms = [t.astype(F32) for t in _split3(lf)] + [jnp.zeros_like(lf)]
        parts = jnp.dot(jnp.concatenate(terms, axis=0).astype(BF16), tri, preferred_element_type=F32)
        cs = parts[0:H_FOX] + parts[H_FOX:2 * H_FOX] + parts[2 * H_FOX:3 * H_FOX] + carry
        ft_ref[0, :, c * chunk:(c + 1) * chunk] = cs
        carry = cs[:, chunk - 1:chunk]


def _cumsum_t(lf_t, chunk=256):
    b, h, s = lf_t.shape
    return pl.pallas_call(
        functools.partial(_cumsum_kernel, chunk=chunk),
        grid=(b,),
        in_specs=[pl.BlockSpec((1, h, s), lambda i: (i, 0, 0))],
        out_specs=pl.BlockSpec((1, h, s), lambda i: (i, 0, 0)),
        out_shape=jax.ShapeDtypeStruct((b, h, s), F32),
        compiler_params=_params(("arbitrary",)),
        name="cumsum",
    )(lf_t)


def _fox_kernel(q_ref, k_ref, v_ref, sg_ref, ft_ref, o_ref, *, t, scale, ndiag):
    qi = pl.program_id(2)
    c2 = scale * LOG2E

    def update(q, k, v, fk2, state, row0):
        m, l, acc = state
        s = lax.dot_general(q, k, NT_DIMS, preferred_element_type=F32) * c2 - fk2
        if row0 is not None:
            r = lax.broadcasted_iota(jnp.int32, s.shape, 0) + row0
            c = lax.broadcasted_iota(jnp.int32, s.shape, 1)
            s = jnp.where(r >= c, s, -jnp.inf)
        m_new = jnp.maximum(m, jnp.max(s, axis=1, keepdims=True))
        alpha = jnp.exp2(m - m_new)
        p = jnp.exp2(s - m_new)
        l = alpha * l + jnp.sum(p, axis=1, keepdims=True)
        acc = alpha * acc + jnp.dot(p.astype(BF16), v, preferred_element_type=F32)
        return m_new, l, acc

    def full_block(kj, state):
        rows = pl.ds(pl.multiple_of(kj * t, t), t)
        return update(q_ref[...], k_ref[rows, :], v_ref[rows, :], ft_ref[0, kj] * LOG2E, state, None)

    init = (jnp.full((t, 1), -jnp.inf, F32), jnp.zeros((t, 1), F32), jnp.zeros((t, HD_FOX), F32))
    m, l, acc = lax.fori_loop(0, qi, full_block, init)

    rb = t // ndiag
    fk_diag = ft_ref[0, qi] * LOG2E
    for band in range(ndiag):
        rows = slice(band * rb, (band + 1) * rb)
        nk = (band + 1) * rb
        keys = pl.ds(pl.multiple_of(qi * t, t), nk)
        mb, lb, accb = update(q_ref[rows, :], k_ref[keys, :], v_ref[keys, :], fk_diag[:, :nk],
                              (m[rows], l[rows], acc[rows]), band * rb)
        o_ref[rows, :] = ((accb / lb) * sg_ref[rows, :].astype(F32)).astype(BF16)


def _fox_prompt(z, f_t, *, batch, seq, t=512):
    nq = seq // t
    cpb = 1024 // HD_FOX
    kern = functools.partial(_fox_kernel, t=t, scale=HD_FOX ** -0.5, ndiag=min(FOX_DIAG_BANDS, t // 128))
    return pl.pallas_call(
        kern,
        grid=(batch, H_FOX, nq),
        in_specs=[
            pl.BlockSpec((t, HD_FOX), lambda b, h, i: (b * nq + i, ZB_QF * cpb + h)),
            pl.BlockSpec((seq, HD_FOX), lambda b, h, i: (b, ZB_KF * cpb + h)),
            pl.BlockSpec((seq, HD_FOX), lambda b, h, i: (b, ZB_VF * cpb + h)),
            pl.BlockSpec((t, HD_FOX), lambda b, h, i: (b * nq + i, ZB_GF * cpb + h)),
            pl.BlockSpec((1, nq, 1, t), lambda b, h, i: (b * H_FOX + h, 0, 0, 0)),
        ],
        out_specs=pl.BlockSpec((t, HD_FOX), lambda b, h, i: (b * nq + i, h)),
        out_shape=jax.ShapeDtypeStruct((batch * seq, W_FOX), BF16),
        compiler_params=_params(("arbitrary", "arbitrary", "arbitrary")),
        name="fox_prompt",
    )(z, z, z, z, f_t)


def _head_norm_gate(o, sg):
    on = o * lax.rsqrt(jnp.mean(o * o, axis=-1, keepdims=True) + EPS)
    return (on * sg.astype(F32)).astype(BF16)


def _ret_kernel(q_ref, k_ref, v_ref, sg_ref, dm_ref, qd_ref, kd_ref, gl_ref, a_ref, st_ref, s_scr, *, chunk):
    s_scr[...] = jnp.zeros_like(s_scr)
    dm = dm_ref[0]
    qd = qd_ref[0]
    kd = kd_ref[0]
    gl = gl_ref[0]
    nc = q_ref.shape[0] // chunk

    def body(c, carry):
        rows = pl.ds(pl.multiple_of(c * chunk, chunk), chunk)
        q = q_ref[rows, :]
        k = k_ref[rows, :]
        v = v_ref[rows, :]
        st = s_scr[...]
        sc = lax.dot_general(q, k, NT_DIMS, preferred_element_type=F32) * dm
        o = (jnp.dot(sc.astype(BF16), v, preferred_element_type=F32)
             + qd * jnp.dot(q, st.astype(BF16), preferred_element_type=F32))
        kdec = (k.astype(F32) * kd).astype(BF16)
        s_scr[...] = gl * st + lax.dot_general(kdec, v, TN_DIMS, preferred_element_type=F32)
        a_ref[rows, :] = _head_norm_gate(o, sg_ref[rows, :])
        return carry

    lax.fori_loop(0, nc, body, 0)
    st_ref[0, 0] = s_scr[...]


def _ret_tables(length, valid):
    lg = jnp.log(1.0 - 2.0 ** (-5.0 - jnp.arange(H_RET, dtype=F32)))
    idx = jnp.arange(length, dtype=F32)
    diff = idx[:, None] - idx[None, :]
    ok = (diff >= 0) & (idx[None, :] < valid)
    dm = jnp.where(ok[None], jnp.exp(lg[:, None, None] * jnp.where(ok, diff, 0.0)[None]), 0.0)
    qd = jnp.exp(lg[:, None] * (idx[None, :] + 1.0))[..., None]
    kd = jnp.where(idx[None, :] < valid, jnp.exp(lg[:, None] * (valid - 1.0 - idx[None, :])), 0.0)[..., None]
    gl = jnp.exp(lg * valid)[:, None, None]
    return dm, qd, kd, gl


def _ret_prompt(z, *, batch, seq, chunk=RET_CHUNK):
    dm, qd, kd, gl = _ret_tables(chunk, chunk)
    cpb = 1024 // DK_RET
    kern = functools.partial(_ret_kernel, chunk=chunk)
    zspec = lambda sec: pl.BlockSpec((seq, DK_RET), lambda b, h: (b, sec * cpb + h))
    tab = lambda shape: pl.BlockSpec((1,) + shape, lambda b, h: (h, 0, 0))
    return pl.pallas_call(
        kern,
        grid=(batch, H_RET),
        in_specs=[zspec(ZB_QR), zspec(ZB_KR), zspec(ZB_VR), zspec(ZB_GR),
                  tab((chunk, chunk)), tab((chunk, 1)), tab((chunk, 1)), tab((1, 1))],
        out_specs=[pl.BlockSpec((seq, DV_RET), lambda b, h: (b, h)),
                   pl.BlockSpec((1, 1, DK_RET, DV_RET), lambda b, h: (b, h, 0, 0))],
        out_shape=[jax.ShapeDtypeStruct((batch * seq, W_RET), BF16),
                   jax.ShapeDtypeStruct((batch, H_RET, DK_RET, DV_RET), F32)],
        scratch_shapes=[pltpu.VMEM((DK_RET, DV_RET), F32)],
        compiler_params=_params(("arbitrary", "arbitrary")),
        name="ret_prompt",
    )(z, z, z, z, dm, qd, kd, gl)


def _ret_sample_kernel(q_ref, k_ref, v_ref, sg_ref, st_ref, dm_ref, qd_ref, kd_ref, gl_ref, a_ref, nst_ref):
    tp = q_ref.shape[1]
    pad = jnp.zeros((PAGE - tp, DK_RET), BF16)
    for h in range(H_RET):
        cols = slice(h * DK_RET, (h + 1) * DK_RET)
        q = q_ref[0, :, cols]
        k = k_ref[0, :, cols]
        v = v_ref[0, :, cols]
        kp = jnp.concatenate([k, pad], axis=0)
        vp = jnp.concatenate([v, pad], axis=0)
        kdp = jnp.concatenate([(k.astype(F32) * kd_ref[h]).astype(BF16), pad], axis=0)
        st = st_ref[0, h]
        sc = lax.dot_general(q, kp, NT_DIMS, preferred_element_type=F32) * dm_ref[h]
        o = (jnp.dot(sc.astype(BF16), vp, preferred_element_type=F32)
             + qd_ref[h] * jnp.dot(q, st.astype(BF16), preferred_element_type=F32))
        nst_ref[0, h] = gl_ref[h] * st + lax.dot_general(kdp, vp, TN_DIMS, preferred_element_type=F32)
        a_ref[0, :, cols] = _head_norm_gate(o, sg_ref[0, :, cols])


def _ret_sample(z3, state, *, valid):
    b, tp, _ = z3.shape
    dm, qd, kd, gl = _ret_tables(tp, valid)
    dm = jnp.pad(dm, ((0, 0), (0, 0), (0, PAGE - tp)))
    zspec = lambda sec: pl.BlockSpec((1, tp, 1024), lambda i: (i, 0, sec))
    full = lambda a: pl.BlockSpec(a.shape, lambda i: (0,) * a.ndim)
    st_spec = pl.BlockSpec((1, H_RET, DK_RET, DV_RET), lambda i: (i, 0, 0, 0))
    return pl.pallas_call(
        _ret_sample_kernel,
        grid=(b,),
        in_specs=[zspec(ZB_QR), zspec(ZB_KR), zspec(ZB_VR), zspec(ZB_GR), st_spec,
                  full(dm), full(qd), full(kd), full(gl)],
        out_specs=[pl.BlockSpec((1, tp, W_RET), lambda i: (i, 0, 0)), st_spec],
        out_shape=[jax.ShapeDtypeStruct((b, tp, W_RET), BF16),
                   jax.ShapeDtypeStruct(state.shape, F32)],
        compiler_params=_params(("arbitrary",)),
        name="ret_sample",
    )(z3, z3, z3, z3, state, dm, qd, kd, gl)


def _fox_bias_kernel(pt_ref, *refs, pps):
    del pt_ref
    lf_refs = refs[:pps]
    o_ref, carry_scr = refs[pps], refs[pps + 1]

    @pl.when(pl.program_id(1) == 0)
    def _():
        carry_scr[...] = jnp.zeros_like(carry_scr)

    lf = jnp.concatenate([r[0] for r in lf_refs], axis=0)
    nl = lf.shape[0]
    jj = lax.broadcasted_iota(jnp.int32, (PAGE, 2 * PAGE), 0)
    ss = lax.broadcasted_iota(jnp.int32, (PAGE, 2 * PAGE), 1)
    same_head = (jj % H_FOX) == (ss % H_FOX)
    w = jnp.where(same_head & ((ss >= PAGE) | (jj // H_FOX > ss // H_FOX)), 1.0, 0.0).astype(BF16)
    parts = jnp.dot(jnp.concatenate(_split3(lf), axis=0), w, preferred_element_type=F32)
    both = parts[0:nl] + parts[nl:2 * nl] + parts[2 * nl:3 * nl]
    within = both[:, :PAGE]
    tot = both[:, PAGE:]
    kdim = max(nl, PAGE)
    ri = lax.broadcasted_iota(jnp.int32, (nl, kdim), 0)
    ci = lax.broadcasted_iota(jnp.int32, (nl, kdim), 1)
    later = jnp.where((ci > ri) & (ci < nl), 1.0, 0.0).astype(BF16)
    t3 = jnp.concatenate(_split3(tot), axis=1)
    if nl < kdim:
        t3 = jnp.concatenate([t3, jnp.zeros((kdim - nl, 3 * PAGE), BF16)], axis=0)
    cross3 = jnp.dot(later, t3, preferred_element_type=F32)
    cross = cross3[:, :PAGE] + cross3[:, PAGE:2 * PAGE] + cross3[:, 2 * PAGE:]
    carry = carry_scr[...]
    r = within + cross + carry
    carry_scr[...] = carry + jnp.sum(tot, axis=0, keepdims=True)
    rows_per_page = PAGE * H_FOX // PAGE
    for pg in range(pps):
        o_ref[0, pg] = r[pg * rows_per_page:(pg + 1) * rows_per_page, :]


def _fox_bias(cache_lf, page_table, *, pps):
    b, n_pages = page_table.shape
    steps = n_pages // pps
    rpp = cache_lf.shape[1]

    def page_map(r):
        return lambda i, p, pt: (pt[i * n_pages + (steps - 1 - p) * pps + r], 0, 0)

    grid_spec = pltpu.PrefetchScalarGridSpec(
        num_scalar_prefetch=1,
        grid=(b, steps),
        in_specs=[pl.BlockSpec((1, rpp, PAGE), page_map(r)) for r in range(pps)],
        out_specs=pl.BlockSpec((1, pps, rpp, PAGE), lambda i, p, pt: (i, steps - 1 - p, 0, 0)),
        scratch_shapes=[pltpu.VMEM((1, PAGE), F32)],
    )
    return pl.pallas_call(
        functools.partial(_fox_bias_kernel, pps=pps),
        grid_spec=grid_spec,
        out_shape=jax.ShapeDtypeStruct((b, n_pages, rpp, PAGE), F32),
        compiler_params=_params(("arbitrary", "arbitrary")),
        name="fox_bias",
    )(page_table.reshape(-1), *([cache_lf] * pps))


def _heads_to_rows(x_row):
    return jnp.concatenate([x_row[:, h * HD_FOX:(h + 1) * HD_FOX] for h in range(H_FOX)], axis=0)


def _fox_sample_kernel(pt_ref, *refs, n_new, scale, pps):
    del pt_ref
    k_refs = refs[0:pps]
    v_refs = refs[pps:2 * pps]
    r_ref, q_ref, kn_ref, vn_ref, sg_ref, lfn_ref, o_ref = refs[2 * pps:2 * pps + 7]
    q_scr, m_scr, l_scr, acc_scr, g_scr = refs[2 * pps + 7:]
    p_id = pl.program_id(1)
    n_rows = n_new * H_FOX
    n_keys = PAGE * H_FOX

    @pl.when(p_id == 0)
    def _():
        q = q_ref[0].astype(F32)
        q_scr[...] = jnp.concatenate([_heads_to_rows(q[t:t + 1, :]) for t in range(n_new)], axis=0).astype(BF16)
        m_scr[...] = jnp.full_like(m_scr, -jnp.inf)
        l_scr[...] = jnp.zeros_like(l_scr)
        acc_scr[...] = jnp.zeros_like(acc_scr)
        lfn = lfn_ref[0]
        r8 = lax.broadcasted_iota(jnp.int32, (H_FOX, H_FOX), 0)
        c8 = lax.broadcasted_iota(jnp.int32, (H_FOX, H_FOX), 1)
        g = jnp.zeros((1, H_FOX), F32)
        cols = []
        for t in range(n_new):
            g = g + lfn[t:t + 1, :]
            cols.append(jnp.sum(jnp.where(r8 == c8, jnp.broadcast_to(g, (H_FOX, H_FOX)), 0.0), axis=1, keepdims=True))
        g_scr[...] = jnp.concatenate(cols, axis=0)

    qm = q_scr[...]
    gcol = g_scr[...]
    row = lax.broadcasted_iota(jnp.int32, (n_rows, n_keys), 0)
    lane = lax.broadcasted_iota(jnp.int32, (n_rows, n_keys), 1)
    own_head = (lane % H_FOX) == (row % H_FOX)

    logits = []
    for r in range(pps):
        kp = k_refs[r][0].reshape(n_keys, HD_FOX).astype(BF16)
        s = lax.dot_general(qm, kp, NT_DIMS, preferred_element_type=F32)
        logits.append(jnp.where(own_head, s * scale + gcol + r_ref[0, r], -jnp.inf))
    m_old = m_scr[...]
    m_new = m_old
    for lg in logits:
        m_new = jnp.maximum(m_new, jnp.max(lg, axis=1, keepdims=True))
    alpha = jnp.exp(m_old - m_new)
    l = alpha * l_scr[...]
    acc = alpha * acc_scr[...]
    for r, lg in enumerate(logits):
        p = jnp.exp(lg - m_new)
        l = l + jnp.sum(p, axis=1, keepdims=True)
        vp = v_refs[r][0].reshape(n_keys, HD_FOX).astype(BF16)
        acc = acc + jnp.dot(p.astype(BF16), vp, preferred_element_type=F32)
    m_scr[...] = m_new
    l_scr[...] = l
    acc_scr[...] = acc

    @pl.when(p_id == pl.num_programs(1) - 1)
    def _():
        qf = q_scr[...].astype(F32)
        kn = kn_ref[0].astype(F32)
        vn = vn_ref[0].astype(F32)
        g_col = g_scr[...]
        tok = lax.broadcasted_iota(jnp.int32, (n_rows, 1), 0) // H_FOX
        tile = lambda x: jnp.concatenate([x] * n_new, axis=0)
        lgs = []
        for t in range(n_new):
            sn = jnp.sum(qf * tile(_heads_to_rows(kn[t:t + 1, :])), axis=1, keepdims=True) * scale
            g_t = tile(g_col[t * H_FOX:(t + 1) * H_FOX, :])
            lgs.append(jnp.where(tok >= t, sn + g_col - g_t, -jnp.inf))
        m0 = m_scr[...]
        m1 = m0
        for lg in lgs:
            m1 = jnp.maximum(m1, lg)
        a1 = jnp.exp(m0 - m1)
        l1 = a1 * l_scr[...]
        acc1 = a1 * acc_scr[...]
        for t, lg in enumerate(lgs):
            pt = jnp.exp(lg - m1)
            l1 = l1 + pt
            acc1 = acc1 + pt * tile(_heads_to_rows(vn[t:t + 1, :]))
        o = acc1 / l1
        outs = [jnp.concatenate([o[t * H_FOX + h:t * H_FOX + h + 1, :] for h in range(H_FOX)], axis=1)
                for t in range(n_new)]
        outs.append(jnp.zeros((o_ref.shape[1] - n_new, W_FOX), F32))
        o_ref[0] = (jnp.concatenate(outs, axis=0) * sg_ref[0].astype(F32)).astype(BF16)


def _fox_sample(z3, lf_new, cache_k, cache_v, bias_rows, page_table, *, n_new, pps):
    b, tp, _ = z3.shape
    n_pages = page_table.shape[1]
    steps = n_pages // pps
    n_rows = n_new * H_FOX

    def page_map(r):
        return lambda i, p, pt: (pt[i * n_pages + (steps - 1 - p) * pps + r], 0, 0, 0)

    kv_specs = [pl.BlockSpec((1, PAGE, H_FOX, HD_FOX), page_map(r)) for r in range(pps)]
    zspec = lambda sec: pl.BlockSpec((1, tp, 1024), lambda i, p, pt: (i, 0, sec))
    kern = functools.partial(_fox_sample_kernel, n_new=n_new, scale=HD_FOX ** -0.5, pps=pps)
    grid_spec = pltpu.PrefetchScalarGridSpec(
        num_scalar_prefetch=1,
        grid=(b, steps),
        in_specs=kv_specs + kv_specs + [
            pl.BlockSpec((1, pps, 1, PAGE * H_FOX), lambda i, p, pt: (i, steps - 1 - p, 0, 0)),
            zspec(ZB_QF), zspec(ZB_KF), zspec(ZB_VF), zspec(ZB_GF),
            pl.BlockSpec((1, tp, H_FOX), lambda i, p, pt: (i, 0, 0)),
        ],
        out_specs=pl.BlockSpec((1, tp, W_FOX), lambda i, p, pt: (i, 0, 0)),
        scratch_shapes=[
            pltpu.VMEM((n_rows, HD_FOX), BF16),
            pltpu.VMEM((n_rows, 1), F32),
            pltpu.VMEM((n_rows, 1), F32),
            pltpu.VMEM((n_rows, HD_FOX), F32),
            pltpu.VMEM((n_rows, 1), F32),
        ],
    )
    return pl.pallas_call(
        kern,
        grid_spec=grid_spec,
        out_shape=jax.ShapeDtypeStruct((b, tp, W_FOX), BF16),
        compiler_params=_params(("arbitrary", "arbitrary")),
        name="fox_sample",
    )(page_table.reshape(-1), *([cache_k] * pps), *([cache_v] * pps), bias_rows, z3, z3, z3, z3, lf_new)


def _merge_kernel(ar_ref, af_ref, mr_ref, mf_ref, x_ref, gate_ref, wb_ref, wo_ref, gf_ref, y_ref):
    p_r = jnp.dot(ar_ref[...], wb_ref[0:W_RET, :], preferred_element_type=F32)
    p_f = jnp.dot(af_ref[...], wb_ref[W_RET:W_RET + W_FOX, :], preferred_element_type=F32)
    merged = mr_ref[...].astype(F32) * p_r + mf_ref[...].astype(F32) * p_f
    out = x_ref[...] + gate_ref[0] * jnp.dot(merged.astype(BF16), wo_ref[...], preferred_element_type=F32)
    y = out * lax.rsqrt(jnp.mean(out * out, axis=-1, keepdims=True) + EPS)
    y_ref[...] = y * gf_ref[...]


def _merge(a_r, a_f, z, x2d, gate, w_branch, w_out, g_final, *, tm):
    rows, d = x2d.shape
    nrb = rows // tm
    groups, r, _ = gate.shape
    bpg = nrb // groups
    return pl.pallas_call(
        _merge_kernel,
        grid=(nrb,),
        in_specs=[
            pl.BlockSpec((tm, W_RET), lambda i: (i, 0)),
            pl.BlockSpec((tm, W_FOX), lambda i: (i, 0)),
            pl.BlockSpec((tm, d), lambda i: (i, 4)),
            pl.BlockSpec((tm, d), lambda i: (i, 5)),
            pl.BlockSpec((tm, d), lambda i: (i, 0)),
            pl.BlockSpec((1, r, d), lambda i: (i // bpg, 0, 0)),
            pl.BlockSpec(w_branch.shape, lambda i: (0, 0)),
            pl.BlockSpec(w_out.shape, lambda i: (0, 0)),
            pl.BlockSpec((1, d), lambda i: (0, 0)),
        ],
        out_specs=pl.BlockSpec((tm, d), lambda i: (i, 0)),
        out_shape=jax.ShapeDtypeStruct((rows, d), F32),
        compiler_params=_params(("arbitrary",)),
        name="merge",
    )(a_r, a_f, z, z, x2d, gate, w_branch, w_out, g_final.reshape(1, d))


def _rope_tables(pos):
    inv = 1.0 / (ROPE_BASE ** (jnp.arange(0, DK_RET, 2, dtype=F32) / DK_RET))
    ang = pos.astype(F32)[:, None] * inv[None, :]
    return jnp.cos(ang), jnp.sin(ang)


def kernel(x_prompt, x_sample, c_prompt, c_sample, cache_k, cache_v, cache_logf, state_ret, page_table,
           w_in, b_fgt, g_norm, w_ada, b_ada, w_branch, w_out, g_final):
    depth = w_in.shape[0]
    assert depth == 1, "the layer loop is written for the single-layer configuration"
    batch, seq, d = x_prompt.shape
    dec_batch, dec_seq, _ = x_sample.shape
    n_pages = page_table.shape[1]
    past_len = n_pages * PAGE
    l = 0

    w_l = w_in[l]
    slabs = lambda w: w.astype(BF16).reshape(d, w.shape[1] // PROJ_WN, PROJ_WN).transpose(1, 0, 2)
    w_lo = slabs(w_l[:, :F_COL0])
    w_hi = slabs(w_l[:, F_COL0 + H_FOX:])
    w_f = jnp.pad(w_l[:, F_COL0:F_COL0 + H_FOX], ((0, 0), (0, 128 - H_FOX))).astype(BF16)
    wb = w_branch[l].astype(BF16)
    wo = w_out[l].astype(BF16)

    mod = _mod(jnp.concatenate([c_prompt, c_sample], axis=0), w_ada[l], b_ada[l])
    shift, scale, gate = jnp.split(mod, 3, axis=-1)

    rows_p = batch * seq
    xp2 = x_prompt.reshape(rows_p, d)
    cos_p, sin_p = _rope_tables(jnp.arange(seq))
    h_p, lf_p = _norm(xp2, scale[:batch, None, :], shift[:batch, None, :], g_norm[l], w_f, b_fgt[l],
                      tm=min(512, seq))
    z_p, kf_p, vf_p = _proj(h_p, w_lo, w_hi, cos_p, sin_p, tm=min(2048, seq))
    t_fox = min(512, seq)
    f_t = _cumsum_t(lf_p.reshape(batch, seq, H_FOX).transpose(0, 2, 1), chunk=min(256, seq))
    a_f_p = _fox_prompt(z_p, f_t.reshape(batch * H_FOX, seq // t_fox, 1, t_fox),
                        batch=batch, seq=seq, t=t_fox)
    a_r_p, st_p = _ret_prompt(z_p, batch=batch, seq=seq)
    y_p = _merge(a_r_p, a_f_p, z_p, xp2, gate[:batch, None, :], wb, wo, g_final, tm=min(256, rows_p))

    rows_s = dec_batch * T_PAD
    xs2 = jnp.pad(x_sample, ((0, 0), (0, T_PAD - dec_seq), (0, 0))).reshape(rows_s, d)
    pos_s = past_len + (jnp.arange(rows_s) % T_PAD)
    cos_s, sin_s = _rope_tables(pos_s)
    tm_s = min(256, rows_s)
    rep = lambda a: jnp.repeat(a[batch:], T_PAD, axis=0).reshape(rows_s // tm_s, tm_s, d)
    h_s, lf_s = _norm(xs2, rep(scale), rep(shift), g_norm[l], w_f, b_fgt[l], tm=tm_s)
    z_s, kf_s, vf_s = _proj(h_s, w_lo, w_hi, cos_s, sin_s, tm=tm_s)
    z_s3 = z_s.reshape(dec_batch, T_PAD, Z_COLS)
    a_r_s, st_s = _ret_sample(z_s3, state_ret[l].astype(F32), valid=dec_seq)
    n_pool = cache_k.shape[1]
    pps = min(PAGES_PER_STEP, n_pages)
    bias = _fox_bias(cache_logf[l].reshape(n_pool, H_FOX, PAGE), page_table,
                     pps=min(BIAS_PAGES_PER_STEP, n_pages))
    a_f_s = _fox_sample(z_s3, lf_s.reshape(dec_batch, T_PAD, H_FOX), cache_k[l], cache_v[l],
                        bias.reshape(dec_batch, n_pages, 1, PAGE * H_FOX), page_table, n_new=dec_seq, pps=pps)
    y_s = _merge(a_r_s.reshape(rows_s, W_RET), a_f_s.reshape(rows_s, W_FOX), z_s, xs2, rep(gate),
                 wb, wo, g_final, tm=tm_s)

    take = lambda a, tail: a.reshape((dec_batch, T_PAD) + tail)[:, :dec_seq]
    return (y_p.reshape(batch, seq, d),
            take(y_s, (d,)),
            kf_p.reshape(1, batch, seq, H_FOX, HD_FOX),
            vf_p.reshape(1, batch, seq, H_FOX, HD_FOX),
            lf_p.reshape(1, batch, seq, H_FOX),
            st_p[None],
            take(kf_s, (H_FOX, HD_FOX))[None],
            take(vf_s, (H_FOX, HD_FOX))[None],
            take(lf_s, (H_FOX,))[None],
            st_s[None])
```

```python
import functools

import jax
import jax.numpy as jnp
import numpy as np
from jax import lax
from jax.experimental import pallas as pl
from jax.experimental.pallas import tpu as pltpu

F32 = jnp.float32
BF16 = jnp.bfloat16

D_MODEL = 2048
H_RET, DK_RET, DV_RET = 4, 256, 256
H_FOX, HD_FOX = 8, 128
W_RET = H_RET * DV_RET
W_FOX = H_FOX * HD_FOX
RET_CHUNK = 512
PAGE = 128
ROPE_BASE = 10000.0
EPS = 1e-6
LOG2E = 1.4426950408889634
T_PAD = 16
PAGES_PER_STEP = 8
BIAS_PAGES_PER_STEP = 32
PROJ_ROW_CHUNK = 512
PROJ_WN = 256
PROJ_W_STREAMS = 2
MERGE_ROW_CHUNK = 256
Z_COLS = 12288
F_COL0 = 8192

ZB_QR, ZB_KR, ZB_VR, ZB_GR, ZB_QF, ZB_KF, ZB_VF, ZB_GF = range(8)
VMEM_LIMIT = 56 * 1024 * 1024

NT_DIMS = (((1,), (1,)), ((), ()))
TN_DIMS = (((0,), (0,)), ((), ()))


def _params(sem):
    return pltpu.CompilerParams(dimension_semantics=sem, vmem_limit_bytes=VMEM_LIMIT)


def _split3(x):
    hi = x.astype(BF16)
    r1 = x - hi.astype(F32)
    mid = r1.astype(BF16)
    lo = (r1 - mid.astype(F32)).astype(BF16)
    return hi, mid, lo


def _mod_kernel(c_ref, *rest, nw):
    w_refs, b_ref, o_ref = rest[:nw], rest[nw], rest[nw + 1]
    c = c_ref[...]
    sc = c * jax.nn.sigmoid(c)
    wn = w_refs[0].shape[1]
    for p, w_ref in enumerate(w_refs):
        cols = slice(p * wn, (p + 1) * wn)
        o_ref[:, cols] = jnp.dot(sc, w_ref[...], preferred_element_type=F32,
                                 precision=lax.Precision.HIGHEST) + b_ref[:, cols]


def _mod(c, w_ada, b_ada, tn=512, nw=1):
    n, d = c.shape
    cols = w_ada.shape[1]
    wn = tn // nw
    return pl.pallas_call(
        functools.partial(_mod_kernel, nw=nw),
        grid=(cols // tn,),
        in_specs=[pl.BlockSpec((n, d), lambda j: (0, 0)),
                  *[pl.BlockSpec((d, wn), lambda j, p=p: (0, j * nw + p)) for p in range(nw)],
                  pl.BlockSpec((1, tn), lambda j: (0, j))],
        out_specs=pl.BlockSpec((n, tn), lambda j: (0, j)),
        out_shape=jax.ShapeDtypeStruct((n, cols), F32),
        compiler_params=_params(("arbitrary",)),
        name="mod",
    )(c, *([w_ada] * nw), b_ada.reshape(1, cols))


def _wprep_kernel(lo_ref, hi_ref, o_ref, *, n_lo):
    q = pl.program_id(0)

    @pl.when(q < n_lo)
    def _():
        o_ref[0] = lo_ref[...].astype(BF16)

    @pl.when(q >= n_lo)
    def _():
        o_ref[0] = hi_ref[...].astype(BF16)


def _wprep(w_full, w_tail, *, n_lo):
    d = w_full.shape[0]
    n_hi = w_tail.shape[1] // PROJ_WN
    return pl.pallas_call(
        functools.partial(_wprep_kernel, n_lo=n_lo),
        grid=(n_lo + n_hi,),
        in_specs=[pl.BlockSpec((d, PROJ_WN), lambda q: (0, jnp.minimum(q, n_lo - 1))),
                  pl.BlockSpec((d, PROJ_WN), lambda q: (0, jnp.clip(q - n_lo, 0, n_hi - 1)))],
        out_specs=pl.BlockSpec((1, d, PROJ_WN), lambda q: (q, 0, 0)),
        out_shape=jax.ShapeDtypeStruct((n_lo + n_hi, d, PROJ_WN), BF16),
        compiler_params=_params(("arbitrary",)),
        name="wprep",
    )(w_full, w_tail)


def _norm_kernel(x_ref, scale_ref, shift_ref, gn_ref, wf_ref, bf_ref, h_ref, lf_ref):
    x = x_ref[...]
    ms = jnp.mean(x * x, axis=-1, keepdims=True)
    xn = x * lax.rsqrt(ms + EPS) * gn_ref[...]
    h = xn * (1.0 + scale_ref[0]) + shift_ref[0]
    hb = h.astype(BF16)
    h_ref[...] = hb
    f = jnp.dot(hb, wf_ref[...], preferred_element_type=F32)
    lf_ref[...] = jax.nn.log_sigmoid(f[:, :H_FOX] + bf_ref[...])


def _norm(x2d, scale, shift, g_norm, w_f, b_f, *, tm):
    rows, d = x2d.shape
    nrb = rows // tm
    groups, r, _ = scale.shape
    bpg = nrb // groups
    return pl.pallas_call(
        _norm_kernel,
        grid=(nrb,),
        in_specs=[
            pl.BlockSpec((tm, d), lambda i: (i, 0)),
            pl.BlockSpec((1, r, d), lambda i: (i // bpg, 0, 0)),
            pl.BlockSpec((1, r, d), lambda i: (i // bpg, 0, 0)),
            pl.BlockSpec((1, d), lambda i: (0, 0)),
            pl.BlockSpec((d, 128), lambda i: (0, 0)),
            pl.BlockSpec((1, H_FOX), lambda i: (0, 0)),
        ],
        out_specs=[pl.BlockSpec((tm, d), lambda i: (i, 0)),
                   pl.BlockSpec((tm, H_FOX), lambda i: (i, 0))],
        out_shape=[jax.ShapeDtypeStruct((rows, d), BF16),
                   jax.ShapeDtypeStruct((rows, H_FOX), F32)],
        compiler_params=_params(("arbitrary",)),
        name="norm",
    )(x2d, scale, shift, g_norm.reshape(1, d), w_f, b_f.reshape(1, H_FOX))


def _proj_kernel(h_ref, *rest, tn, rc_rows, nw, n_lo_blocks):
    lo_refs, hi_refs = rest[:nw], rest[nw:2 * nw]
    cos_ref, sin_ref, z_ref, kf_ref, vf_ref = rest[2 * nw:]
    wn = tn // nw
    j = pl.program_id(1)
    per_kb = 1024 // tn
    sec = j // per_kb
    tm = h_ref.shape[0]
    rc = min(rc_rows, tm)

    def run(epilogue, w_refs):
        for c0 in range(0, tm, rc):
            rows = slice(c0, c0 + rc)
            for p, w_ref in enumerate(w_refs):
                cols = slice(p * wn, (p + 1) * wn)
                epilogue(jnp.dot(h_ref[rows, :], w_ref[0], preferred_element_type=F32), rows, cols)

    def rope_epilogue(acc, rows, cols):
        c = cos_ref[rows, :]
        s = sin_ref[rows, :]
        sc = jnp.where(sec == ZB_KR, DK_RET ** -0.5, 1.0).astype(F32)
        for hh in range(wn // DK_RET):
            lo, mid, hi = hh * DK_RET, hh * DK_RET + DK_RET // 2, (hh + 1) * DK_RET
            x1 = acc[:, lo:mid]
            x2 = acc[:, mid:hi]
            z_ref[rows, cols.start + lo:cols.start + mid] = ((x1 * c - x2 * s) * sc).astype(BF16)
            z_ref[rows, cols.start + mid:cols.start + hi] = ((x1 * s + x2 * c) * sc).astype(BF16)

    def kf_epilogue(acc, rows, cols):
        z_ref[rows, cols] = acc.astype(BF16)
        kf_ref[rows, cols] = acc

    def vf_epilogue(acc, rows, cols):
        z_ref[rows, cols] = acc.astype(BF16)
        vf_ref[rows, cols] = acc

    is_silu = (sec == ZB_GR) | (sec == ZB_GF)
    ca = jnp.where(is_silu, 0.0, 1.0).astype(F32)
    cb = jnp.where(is_silu, 1.0, 0.0).astype(F32)

    def plain_or_silu_epilogue(acc, rows, cols):
        z_ref[rows, cols] = (acc * ca + jax.nn.sigmoid(acc) * (acc * cb)).astype(BF16)

    def sigmoid_epilogue(acc, rows, cols):
        z_ref[rows, cols] = jax.nn.sigmoid(acc).astype(BF16)

    is_rope = sec <= ZB_KR
    is_hi = j >= n_lo_blocks
    pl.when(is_rope)(lambda: run(rope_epilogue, lo_refs))
    pl.when(sec == ZB_KF)(lambda: run(kf_epilogue, lo_refs))
    pl.when(sec == ZB_VF)(lambda: run(vf_epilogue, lo_refs))
    pl.when(is_hi)(lambda: run(sigmoid_epilogue, hi_refs))
    pl.when(jnp.logical_not(is_rope | is_hi) & (sec != ZB_KF) & (sec != ZB_VF))(
        lambda: run(plain_or_silu_epilogue, lo_refs))


def _proj(h, w_slabs, cos, sin, *, tm, n_lo):
    rows, d = h.shape
    nw = PROJ_W_STREAMS
    _, _, wn = w_slabs.shape
    n_hi = w_slabs.shape[0] - n_lo
    tn = nw * wn
    nrb = rows // tm
    ncb = Z_COLS // tn
    per_kb = 1024 // tn
    n_lo_blocks = n_lo // nw
    ntab = cos.shape[0] // tm
    kern = functools.partial(_proj_kernel, tn=tn, rc_rows=PROJ_ROW_CHUNK, nw=nw, n_lo_blocks=n_lo_blocks)
    kf0, vf0 = ZB_KF * per_kb, ZB_VF * per_kb
    return pl.pallas_call(
        kern,
        grid=(nrb, ncb),
        in_specs=[
            pl.BlockSpec((tm, d), lambda i, j: (i, 0)),
            *[pl.BlockSpec((1, d, wn), lambda i, j, p=p: (jnp.minimum(j * nw + p, n_lo - 1), 0, 0))
              for p in range(nw)],
            *[pl.BlockSpec((1, d, wn), lambda i, j, p=p: (n_lo + jnp.clip(j * nw + p - n_lo, 0, n_hi - 1), 0, 0))
              for p in range(nw)],
            pl.BlockSpec((tm, 128), lambda i, j: (i % ntab, 0)),
            pl.BlockSpec((tm, 128), lambda i, j: (i % ntab, 0)),
        ],
        out_specs=[
            pl.BlockSpec((tm, tn), lambda i, j: (i, j)),
            pl.BlockSpec((tm, tn), lambda i, j: (i, jnp.clip(j - kf0, 0, per_kb - 1))),
            pl.BlockSpec((tm, tn), lambda i, j: (i, jnp.clip(j - vf0, 0, per_kb - 1))),
        ],
        out_shape=[
            jax.ShapeDtypeStruct((rows, Z_COLS), BF16),
            jax.ShapeDtypeStruct((rows, W_FOX), F32),
            jax.ShapeDtypeStruct((rows, W_FOX), F32),
        ],
        compiler_params=_params(("arbitrary", "arbitrary")),
        name="proj",
    )(h, *([w_slabs] * 2 * nw), cos, sin)


def _cumsum_kernel(lt_ref, ft_ref, *, chunk):
    s = lt_ref.shape[2]
    row = lax.broadcasted_iota(jnp.int32, (chunk, chunk), 0)
    col = lax.broadcasted_iota(jnp.int32, (chunk, chunk), 1)
    tri = jnp.where(row <= col, 1.0, 0.0).astype(BF16)
    carry = jnp.zeros((H_FOX, 1), F32)
    for c in range(s // chunk):
        lf = lt_ref[0, :, c * chunk:(c + 1) * chunk]
        terms = [t.astype(F32) for t in _split3(lf)] + [jnp.zeros_like(lf)]
        parts = jnp.dot(jnp.concatenate(terms, axis=0).astype(BF16), tri, preferred_element_type=F32)
        cs = parts[0:H_FOX] + parts[H_FOX:2 * H_FOX] + parts[2 * H_FOX:3 * H_FOX] + carry
        ft_ref[0, :, c * chunk:(c + 1) * chunk] = cs
        carry = cs[:, chunk - 1:chunk]


def _cumsum_t(lf_t, chunk=256):
    b, h, s = lf_t.shape
    return pl.pallas_call(
        functools.partial(_cumsum_kernel, chunk=chunk),
        grid=(b,),
        in_specs=[pl.BlockSpec((1, h, s), lambda i: (i, 0, 0))],
        out_specs=pl.BlockSpec((1, h, s), lambda i: (i, 0, 0)),
        out_shape=jax.ShapeDtypeStruct((b, h, s), F32),
        compiler_params=_params(("arbitrary",)),
        name="cumsum",
    )(lf_t)


def _fox_kernel(q_ref, k_ref, v_ref, sg_ref, ft_ref, o_ref, *, t, scale):
    qi = pl.program_id(2)
    c2 = scale * LOG2E
    q = q_ref[...]

    def block(kj, state, masked):
        m, l, acc = state
        rows = pl.ds(pl.multiple_of(kj * t, t), t)
        fk2 = ft_ref[0, kj] * LOG2E
        s = lax.dot_general(q, k_ref[rows, :], NT_DIMS, preferred_element_type=F32) * c2 - fk2
        if masked:
            r = lax.broadcasted_iota(jnp.int32, s.shape, 0)
            c = lax.broadcasted_iota(jnp.int32, s.shape, 1)
            s = jnp.where(r >= c, s, -jnp.inf)
        m_new = jnp.maximum(m, jnp.max(s, axis=1, keepdims=True))
        alpha = jnp.exp2(m - m_new)
        p = jnp.exp2(s - m_new)
        l = alpha * l + jnp.sum(p, axis=1, keepdims=True)
        acc = alpha * acc + jnp.dot(p.astype(BF16), v_ref[rows, :], preferred_element_type=F32)
        return m_new, l, acc

    init = (jnp.full((t, 1), -jnp.inf, F32), jnp.zeros((t, 1), F32), jnp.zeros((t, HD_FOX), F32))
    state = lax.fori_loop(0, qi, lambda kj, st: block(kj, st, False), init)
    _, l, acc = block(qi, state, True)
    o_ref[...] = ((acc / l) * sg_ref[...].astype(F32)).astype(BF16)


def _fox_prompt(z, f_t, *, batch, seq, t=512):
    nq = seq // t
    cpb = 1024 // HD_FOX
    kern = functools.partial(_fox_kernel, t=t, scale=HD_FOX ** -0.5)
    return pl.pallas_call(
        kern,
        grid=(batch, H_FOX, nq),
        in_specs=[
            pl.BlockSpec((t, HD_FOX), lambda b, h, i: (b * nq + i, ZB_QF * cpb + h)),
            pl.BlockSpec((seq, HD_FOX), lambda b, h, i: (b, ZB_KF * cpb + h)),
            pl.BlockSpec((seq, HD_FOX), lambda b, h, i: (b, ZB_VF * cpb + h)),
            pl.BlockSpec((t, HD_FOX), lambda b, h, i: (b * nq + i, ZB_GF * cpb + h)),
            pl.BlockSpec((1, nq, 1, t), lambda b, h, i: (b * H_FOX + h, 0, 0, 0)),
        ],
        out_specs=pl.BlockSpec((t, HD_FOX), lambda b, h, i: (b * nq + i, h)),
        out_shape=jax.ShapeDtypeStruct((batch * seq, W_FOX), BF16),
        compiler_params=_params(("arbitrary", "arbitrary", "arbitrary")),
        name="fox_prompt",
    )(z, z, z, z, f_t)


def _head_norm_gate(o, sg):
    on = o * lax.rsqrt(jnp.mean(o * o, axis=-1, keepdims=True) + EPS)
    return (on * sg.astype(F32)).astype(BF16)


def _ret_kernel(q_ref, k_ref, v_ref, sg_ref, dm_ref, qd_ref, kd_ref, gl_ref, a_ref, st_ref, s_scr, *, chunk):
    s_scr[...] = jnp.zeros_like(s_scr)
    dm = dm_ref[0]
    qd = qd_ref[0]
    kd = kd_ref[0]
    gl = gl_ref[0]
    nc = q_ref.shape[0] // chunk

    def body(c, carry):
        rows = pl.ds(pl.multiple_of(c * chunk, chunk), chunk)
        q = q_ref[rows, :]
        k = k_ref[rows, :]
        v = v_ref[rows, :]
        st = s_scr[...]
        sc = lax.dot_general(q, k, NT_DIMS, preferred_element_type=F32) * dm
        o = (jnp.dot(sc.astype(BF16), v, preferred_element_type=F32)
             + qd * jnp.dot(q, st.astype(BF16), preferred_element_type=F32))
        kdec = (k.astype(F32) * kd).astype(BF16)
        s_scr[...] = gl * st + lax.dot_general(kdec, v, TN_DIMS, preferred_element_type=F32)
        a_ref[rows, :] = _head_norm_gate(o, sg_ref[rows, :])
        return carry

    lax.fori_loop(0, nc, body, 0)
    st_ref[0, 0] = s_scr[...]


def _ret_tables(length, valid):
    lg = jnp.log(1.0 - 2.0 ** (-5.0 - jnp.arange(H_RET, dtype=F32)))
    idx = jnp.arange(length, dtype=F32)
    diff = idx[:, None] - idx[None, :]
    ok = (diff >= 0) & (idx[None, :] < valid)
    dm = jnp.where(ok[None], jnp.exp(lg[:, None, None] * jnp.where(ok, diff, 0.0)[None]), 0.0)
    qd = jnp.exp(lg[:, None] * (idx[None, :] + 1.0))[..., None]
    kd = jnp.where(idx[None, :] < valid, jnp.exp(lg[:, None] * (valid - 1.0 - idx[None, :])), 0.0)[..., None]
    gl = jnp.exp(lg * valid)[:, None, None]
    return dm, qd, kd, gl


def _ret_prompt(z, *, batch, seq, chunk=RET_CHUNK):
    dm, qd, kd, gl = _ret_tables(chunk, chunk)
    cpb = 1024 // DK_RET
    kern = functools.partial(_ret_kernel, chunk=chunk)
    zspec = lambda sec: pl.BlockSpec((seq, DK_RET), lambda b, h: (b, sec * cpb + h))
    tab = lambda shape: pl.BlockSpec((1,) + shape, lambda b, h: (h, 0, 0))
    return pl.pallas_call(
        kern,
        grid=(batch, H_RET),
        in_specs=[zspec(ZB_QR), zspec(ZB_KR), zspec(ZB_VR), zspec(ZB_GR),
                  tab((chunk, chunk)), tab((chunk, 1)), tab((chunk, 1)), tab((1, 1))],
        out_specs=[pl.BlockSpec((seq, DV_RET), lambda b, h: (b, h)),
                   pl.BlockSpec((1, 1, DK_RET, DV_RET), lambda b, h: (b, h, 0, 0))],
        out_shape=[jax.ShapeDtypeStruct((batch * seq, W_RET), BF16),
                   jax.ShapeDtypeStruct((batch, H_RET, DK_RET, DV_RET), F32)],
        scratch_shapes=[pltpu.VMEM((DK_RET, DV_RET), F32)],
        compiler_params=_params(("arbitrary", "arbitrary")),
        name="ret_prompt",
    )(z, z, z, z, dm, qd, kd, gl)


def _ret_sample_kernel(q_ref, k_ref, v_ref, sg_ref, st_ref, dm_ref, qd_ref, kd_ref, gl_ref, a_ref, nst_ref):
    tp = q_ref.shape[1]
    pad = jnp.zeros((PAGE - tp, DK_RET), BF16)
    for h in range(H_RET):
        cols = slice(h * DK_RET, (h + 1) * DK_RET)
        q = q_ref[0, :, cols]
        k = k_ref[0, :, cols]
        v = v_ref[0, :, cols]
        kp = jnp.concatenate([k, pad], axis=0)
        vp = jnp.concatenate([v, pad], axis=0)
        kdp = jnp.concatenate([(k.astype(F32) * kd_ref[h]).astype(BF16), pad], axis=0)
        st = st_ref[0, h]
        sc = lax.dot_general(q, kp, NT_DIMS, preferred_element_type=F32) * dm_ref[h]
        o = (jnp.dot(sc.astype(BF16), vp, preferred_element_type=F32)
             + qd_ref[h] * jnp.dot(q, st.astype(BF16), preferred_element_type=F32))
        nst_ref[0, h] = gl_ref[h] * st + lax.dot_general(kdp, vp, TN_DIMS, preferred_element_type=F32)
        a_ref[0, :, cols] = _head_norm_gate(o, sg_ref[0, :, cols])


def _ret_sample(z3, state, *, valid):
    b, tp, _ = z3.shape
    dm, qd, kd, gl = _ret_tables(tp, valid)
    dm = jnp.pad(dm, ((0, 0), (0, 0), (0, PAGE - tp)))
    zspec = lambda sec: pl.BlockSpec((1, tp, 1024), lambda i: (i, 0, sec))
    full = lambda a: pl.BlockSpec(a.shape, lambda i: (0,) * a.ndim)
    st_spec = pl.BlockSpec((1, H_RET, DK_RET, DV_RET), lambda i: (i, 0, 0, 0))
    return pl.pallas_call(
        _ret_sample_kernel,
        grid=(b,),
        in_specs=[zspec(ZB_QR), zspec(ZB_KR), zspec(ZB_VR), zspec(ZB_GR), st_spec,
                  full(dm), full(qd), full(kd), full(gl)],
        out_specs=[pl.BlockSpec((1, tp, W_RET), lambda i: (i, 0, 0)), st_spec],
        out_shape=[jax.ShapeDtypeStruct((b, tp, W_RET), BF16),
                   jax.ShapeDtypeStruct(state.shape, F32)],
        compiler_params=_params(("arbitrary",)),
        name="ret_sample",
    )(z3, z3, z3, z3, state, dm, qd, kd, gl)


def _fox_bias_kernel(pt_ref, *refs, pps):
    del pt_ref
    lf_refs = refs[:pps]
    o_ref, carry_scr = refs[pps], refs[pps + 1]

    @pl.when(pl.program_id(1) == 0)
    def _():
        carry_scr[...] = jnp.zeros_like(carry_scr)

    lf = jnp.concatenate([r[0] for r in lf_refs], axis=0)
    nl = lf.shape[0]
    jj = lax.broadcasted_iota(jnp.int32, (PAGE, 2 * PAGE), 0)
    ss = lax.broadcasted_iota(jnp.int32, (PAGE, 2 * PAGE), 1)
    same_head = (jj % H_FOX) == (ss % H_FOX)
    w = jnp.where(same_head & ((ss >= PAGE) | (jj // H_FOX > ss // H_FOX)), 1.0, 0.0).astype(BF16)
    parts = jnp.dot(jnp.concatenate(_split3(lf), axis=0), w, preferred_element_type=F32)
    both = parts[0:nl] + parts[nl:2 * nl] + parts[2 * nl:3 * nl]
    within = both[:, :PAGE]
    tot = both[:, PAGE:]
    kdim = max(nl, PAGE)
    ri = lax.broadcasted_iota(jnp.int32, (nl, kdim), 0)
    ci = lax.broadcasted_iota(jnp.int32, (nl, kdim), 1)
    later = jnp.where((ci > ri) & (ci < nl), 1.0, 0.0).astype(BF16)
    t3 = jnp.concatenate(_split3(tot), axis=1)
    if nl < kdim:
        t3 = jnp.concatenate([t3, jnp.zeros((kdim - nl, 3 * PAGE), BF16)], axis=0)
    cross3 = jnp.dot(later, t3, preferred_element_type=F32)
    cross = cross3[:, :PAGE] + cross3[:, PAGE:2 * PAGE] + cross3[:, 2 * PAGE:]
    carry = carry_scr[...]
    r = within + cross + carry
    carry_scr[...] = carry + jnp.sum(tot, axis=0, keepdims=True)
    rows_per_page = PAGE * H_FOX // PAGE
    for pg in range(pps):
        o_ref[0, pg] = r[pg * rows_per_page:(pg + 1) * rows_per_page, :]


def _fox_bias(cache_lf, page_table, *, pps):
    b, n_pages = page_table.shape
    steps = n_pages // pps
    rpp = cache_lf.shape[1]

    def page_map(r):
        return lambda i, p, pt: (pt[i * n_pages + (steps - 1 - p) * pps + r], 0, 0)

    grid_spec = pltpu.PrefetchScalarGridSpec(
        num_scalar_prefetch=1,
        grid=(b, steps),
        in_specs=[pl.BlockSpec((1, rpp, PAGE), page_map(r)) for r in range(pps)],
        out_specs=pl.BlockSpec((1, pps, rpp, PAGE), lambda i, p, pt: (i, steps - 1 - p, 0, 0)),
        scratch_shapes=[pltpu.VMEM((1, PAGE), F32)],
    )
    return pl.pallas_call(
        functools.partial(_fox_bias_kernel, pps=pps),
        grid_spec=grid_spec,
        out_shape=jax.ShapeDtypeStruct((b, n_pages, rpp, PAGE), F32),
        compiler_params=_params(("arbitrary", "arbitrary")),
        name="fox_bias",
    )(page_table.reshape(-1), *([cache_lf] * pps))


def _heads_to_rows(x_row):
    return jnp.concatenate([x_row[:, h * HD_FOX:(h + 1) * HD_FOX] for h in range(H_FOX)], axis=0)


def _fox_sample_kernel(pt_ref, *refs, n_new, scale, pps):
    del pt_ref
    k_refs = refs[0:pps]
    v_refs = refs[pps:2 * pps]
    r_ref, q_ref, kn_ref, vn_ref, sg_ref, lfn_ref, o_ref = refs[2 * pps:2 * pps + 7]
    q_scr, m_scr, l_scr, acc_scr, g_scr = refs[2 * pps + 7:]
    p_id = pl.program_id(1)
    n_rows = n_new * H_FOX
    n_keys = PAGE * H_FOX

    @pl.when(p_id == 0)
    def _():
        q = q_ref[0].astype(F32)
        q_scr[...] = jnp.concatenate([_heads_to_rows(q[t:t + 1, :]) for t in range(n_new)], axis=0).astype(BF16)
        m_scr[...] = jnp.full_like(m_scr, -jnp.inf)
        l_scr[...] = jnp.zeros_like(l_scr)
        acc_scr[...] = jnp.zeros_like(acc_scr)
        lfn = lfn_ref[0]
        r8 = lax.broadcasted_iota(jnp.int32, (H_FOX, H_FOX), 0)
        c8 = lax.broadcasted_iota(jnp.int32, (H_FOX, H_FOX), 1)
        g = jnp.zeros((1, H_FOX), F32)
        cols = []
        for t in range(n_new):
            g = g + lfn[t:t + 1, :]
            cols.append(jnp.sum(jnp.where(r8 == c8, jnp.broadcast_to(g, (H_FOX, H_FOX)), 0.0), axis=1, keepdims=True))
        g_scr[...] = jnp.concatenate(cols, axis=0)

    qm = q_scr[...]
    gcol = g_scr[...]
    row = lax.broadcasted_iota(jnp.int32, (n_rows, n_keys), 0)
    lane = lax.broadcasted_iota(jnp.int32, (n_rows, n_keys), 1)
    own_head = (lane % H_FOX) == (row % H_FOX)

    logits = []
    for r in range(pps):
        kp = k_refs[r][0].reshape(n_keys, HD_FOX).astype(BF16)
        s = lax.dot_general(qm, kp, NT_DIMS, preferred_element_type=F32)
        logits.append(jnp.where(own_head, s * scale + gcol + r_ref[0, r], -jnp.inf))
    m_old = m_scr[...]
    m_new = m_old
    for lg in logits:
        m_new = jnp.maximum(m_new, jnp.max(lg, axis=1, keepdims=True))
    alpha = jnp.exp(m_old - m_new)
    l = alpha * l_scr[...]
    acc = alpha * acc_scr[...]
    for r, lg in enumerate(logits):
        p = jnp.exp(lg - m_new)
        l = l + jnp.sum(p, axis=1, keepdims=True)
        vp = v_refs[r][0].reshape(n_keys, HD_FOX).astype(BF16)
        acc = acc + jnp.dot(p.astype(BF16), vp, preferred_element_type=F32)
    m_scr[...] = m_new
    l_scr[...] = l
    acc_scr[...] = acc

    @pl.when(p_id == pl.num_programs(1) - 1)
    def _():
        qf = q_scr[...].astype(F32)
        kn = kn_ref[0].astype(F32)
        vn = vn_ref[0].astype(F32)
        g_col = g_scr[...]
        tok = lax.broadcasted_iota(jnp.int32, (n_rows, 1), 0) // H_FOX
        tile = lambda x: jnp.concatenate([x] * n_new, axis=0)
        lgs = []
        for t in range(n_new):
            sn = jnp.sum(qf * tile(_heads_to_rows(kn[t:t + 1, :])), axis=1, keepdims=True) * scale
            g_t = tile(g_col[t * H_FOX:(t + 1) * H_FOX, :])
            lgs.append(jnp.where(tok >= t, sn + g_col - g_t, -jnp.inf))
        m0 = m_scr[...]
        m1 = m0
        for lg in lgs:
            m1 = jnp.maximum(m1, lg)
        a1 = jnp.exp(m0 - m1)
        l1 = a1 * l_scr[...]
        acc1 = a1 * acc_scr[...]
        for t, lg in enumerate(lgs):
            pt = jnp.exp(lg - m1)
            l1 = l1 + pt
            acc1 = acc1 + pt * tile(_heads_to_rows(vn[t:t + 1, :]))
        o = acc1 / l1
        outs = [jnp.concatenate([o[t * H_FOX + h:t * H_FOX + h + 1, :] for h in range(H_FOX)], axis=1)
                for t in range(n_new)]
        outs.append(jnp.zeros((o_ref.shape[1] - n_new, W_FOX), F32))
        o_ref[0] = (jnp.concatenate(outs, axis=0) * sg_ref[0].astype(F32)).astype(BF16)


def _fox_sample(z3, lf_new, cache_k, cache_v, bias_rows, page_table, *, n_new, pps):
    b, tp, _ = z3.shape
    n_pages = page_table.shape[1]
    steps = n_pages // pps
    n_rows = n_new * H_FOX

    def page_map(r):
        return lambda i, p, pt: (pt[i * n_pages + (steps - 1 - p) * pps + r], 0, 0, 0)

    kv_specs = [pl.BlockSpec((1, PAGE, H_FOX, HD_FOX), page_map(r)) for r in range(pps)]
    zspec = lambda sec: pl.BlockSpec((1, tp, 1024), lambda i, p, pt: (i, 0, sec))
    kern = functools.partial(_fox_sample_kernel, n_new=n_new, scale=HD_FOX ** -0.5, pps=pps)
    grid_spec = pltpu.PrefetchScalarGridSpec(
        num_scalar_prefetch=1,
        grid=(b, steps),
        in_specs=kv_specs + kv_specs + [
            pl.BlockSpec((1, pps, 1, PAGE * H_FOX), lambda i, p, pt: (i, steps - 1 - p, 0, 0)),
            zspec(ZB_QF), zspec(ZB_KF), zspec(ZB_VF), zspec(ZB_GF),
            pl.BlockSpec((1, tp, H_FOX), lambda i, p, pt: (i, 0, 0)),
        ],
        out_specs=pl.BlockSpec((1, tp, W_FOX), lambda i, p, pt: (i, 0, 0)),
        scratch_shapes=[
            pltpu.VMEM((n_rows, HD_FOX), BF16),
            pltpu.VMEM((n_rows, 1), F32),
            pltpu.VMEM((n_rows, 1), F32),
            pltpu.VMEM((n_rows, HD_FOX), F32),
            pltpu.VMEM((n_rows, 1), F32),
        ],
    )
    return pl.pallas_call(
        kern,
        grid_spec=grid_spec,
        out_shape=jax.ShapeDtypeStruct((b, tp, W_FOX), BF16),
        compiler_params=_params(("arbitrary", "arbitrary")),
        name="fox_sample",
    )(page_table.reshape(-1), *([cache_k] * pps), *([cache_v] * pps), bias_rows, z3, z3, z3, z3, lf_new)


def _merge_kernel(ar_ref, af_ref, mr_ref, mf_ref, x_ref, gate_ref, wb_ref, wo_ref, gf_ref, y_ref, *, rc):
    tm = x_ref.shape[0]
    for c0 in range(0, tm, rc):
        rows = slice(c0, c0 + rc)
        p_r = jnp.dot(ar_ref[rows, :], wb_ref[0:W_RET, :], preferred_element_type=F32)
        p_f = jnp.dot(af_ref[rows, :], wb_ref[W_RET:W_RET + W_FOX, :], preferred_element_type=F32)
        merged = mr_ref[rows, :].astype(F32) * p_r + mf_ref[rows, :].astype(F32) * p_f
        gate = gate_ref[0] if gate_ref.shape[1] == 1 else gate_ref[0, rows, :]
        out = x_ref[rows, :] + gate * jnp.dot(merged.astype(BF16), wo_ref[...], preferred_element_type=F32)
        y = out * lax.rsqrt(jnp.mean(out * out, axis=-1, keepdims=True) + EPS)
        y_ref[rows, :] = y * gf_ref[...]


def _merge(a_r, a_f, z, x2d, gate, w_branch, w_out, g_final, *, tm):
    rows, d = x2d.shape
    nrb = rows // tm
    groups, r, _ = gate.shape
    bpg = nrb // groups
    return pl.pallas_call(
        functools.partial(_merge_kernel, rc=min(MERGE_ROW_CHUNK, tm)),
        grid=(nrb,),
        in_specs=[
            pl.BlockSpec((tm, W_RET), lambda i: (i, 0)),
            pl.BlockSpec((tm, W_FOX), lambda i: (i, 0)),
            pl.BlockSpec((tm, d), lambda i: (i, 4)),
            pl.BlockSpec((tm, d), lambda i: (i, 5)),
            pl.BlockSpec((tm, d), lambda i: (i, 0)),
            pl.BlockSpec((1, r, d), lambda i: (i // bpg, 0, 0)),
            pl.BlockSpec(w_branch.shape, lambda i: (0, 0)),
            pl.BlockSpec(w_out.shape, lambda i: (0, 0)),
            pl.BlockSpec((1, d), lambda i: (0, 0)),
        ],
        out_specs=pl.BlockSpec((tm, d), lambda i: (i, 0)),
        out_shape=jax.ShapeDtypeStruct((rows, d), F32),
        compiler_params=_params(("arbitrary",)),
        name="merge",
    )(a_r, a_f, z, z, x2d, gate, w_branch, w_out, g_final.reshape(1, d))


def _rope_tables(pos):
    inv = 1.0 / (ROPE_BASE ** (jnp.arange(0, DK_RET, 2, dtype=F32) / DK_RET))
    ang = pos.astype(F32)[:, None] * inv[None, :]
    return jnp.cos(ang), jnp.sin(ang)


def kernel(x_prompt, x_sample, c_prompt, c_sample, cache_k, cache_v, cache_logf, state_ret, page_table,
           w_in, b_fgt, g_norm, w_ada, b_ada, w_branch, w_out, g_final):
    depth = w_in.shape[0]
    assert depth == 1, "the layer loop is written for the single-layer configuration"
    batch, seq, d = x_prompt.shape
    dec_batch, dec_seq, _ = x_sample.shape
    n_pages = page_table.shape[1]
    past_len = n_pages * PAGE
    l = 0

    w_l = w_in[l]
    w_slabs = _wprep(w_l, w_l[:, F_COL0 + H_FOX:], n_lo=F_COL0 // PROJ_WN)
    w_f = jnp.pad(w_l[:, F_COL0:F_COL0 + H_FOX], ((0, 0), (0, 128 - H_FOX))).astype(BF16)
    wb = w_branch[l].astype(BF16)
    wo = w_out[l].astype(BF16)

    mod = _mod(jnp.concatenate([c_prompt, c_sample], axis=0), w_ada[l], b_ada[l])
    shift, scale, gate = jnp.split(mod, 3, axis=-1)

    rows_p = batch * seq
    xp2 = x_prompt.reshape(rows_p, d)
    cos_p, sin_p = _rope_tables(jnp.arange(seq))
    h_p, lf_p = _norm(xp2, scale[:batch, None, :], shift[:batch, None, :], g_norm[l], w_f, b_fgt[l],
                      tm=min(512, seq))
    z_p, kf_p, vf_p = _proj(h_p, w_slabs, cos_p, sin_p, tm=min(2048, seq), n_lo=F_COL0 // PROJ_WN)
    t_fox = min(1024, seq)
    f_t = _cumsum_t(lf_p.reshape(batch, seq, H_FOX).transpose(0, 2, 1), chunk=min(256, seq))
    a_f_p = _fox_prompt(z_p, f_t.reshape(batch * H_FOX, seq // t_fox, 1, t_fox),
                        batch=batch, seq=seq, t=t_fox)
    a_r_p, st_p = _ret_prompt(z_p, batch=batch, seq=seq)
    y_p = _merge(a_r_p, a_f_p, z_p, xp2, gate[:batch, None, :], wb, wo, g_final, tm=min(256, rows_p))

    rows_s = dec_batch * T_PAD
    xs2 = jnp.pad(x_sample, ((0, 0), (0, T_PAD - dec_seq), (0, 0))).reshape(rows_s, d)
    pos_s = past_len + (jnp.arange(rows_s) % T_PAD)
    cos_s, sin_s = _rope_tables(pos_s)
    tm_s = min(256, rows_s)
    rep = lambda a: jnp.repeat(a[batch:], T_PAD, axis=0).reshape(rows_s // tm_s, tm_s, d)
    h_s, lf_s = _norm(xs2, rep(scale), rep(shift), g_norm[l], w_f, b_fgt[l], tm=tm_s)
    z_s, kf_s, vf_s = _proj(h_s, w_slabs, cos_s, sin_s, tm=tm_s, n_lo=F_COL0 // PROJ_WN)
    z_s3 = z_s.reshape(dec_batch, T_PAD, Z_COLS)
    a_r_s, st_s = _ret_sample(z_s3, state_ret[l].astype(F32), valid=dec_seq)
    n_pool = cache_k.shape[1]
    pps = min(PAGES_PER_STEP, n_pages)
    bias = _fox_bias(cache_logf[l].reshape(n_pool, H_FOX, PAGE), page_table,
                     pps=min(BIAS_PAGES_PER_STEP, n_pages))
    a_f_s = _fox_sample(z_s3, lf_s.reshape(dec_batch, T_PAD, H_FOX), cache_k[l], cache_v[l],
                        bias.reshape(dec_batch, n_pages, 1, PAGE * H_FOX), page_table, n_new=dec_seq, pps=pps)
    y_s = _merge(a_r_s.reshape(rows_s, W_RET), a_f_s.reshape(rows_s, W_FOX), z_s, xs2, rep(gate),
                 wb, wo, g_final, tm=tm_s)

    take = lambda a, tail: a.reshape((dec_batch, T_PAD) + tail)[:, :dec_seq]
    return (y_p.reshape(batch, seq, d),
            take(y_s, (d,)),
            kf_p.reshape(1, batch, seq, H_FOX, HD_FOX),
            vf_p.reshape(1, batch, seq, H_FOX, HD_FOX),
            lf_p.reshape(1, batch, seq, H_FOX),
            st_p[None],
            take(kf_s, (H_FOX, HD_FOX))[None],
            take(vf_s, (H_FOX, HD_FOX))[None],
            take(lf_s, (H_FOX,))[None],
            st_s[None])
```

```python
import functools

import jax
import jax.numpy as jnp
import numpy as np
from jax import lax
from jax.experimental import pallas as pl
from jax.experimental.pallas import tpu as pltpu

F32 = jnp.float32
BF16 = jnp.bfloat16

D_MODEL = 2048
H_RET, DK_RET, DV_RET = 4, 256, 256
H_FOX, HD_FOX = 8, 128
W_RET = H_RET * DV_RET
W_FOX = H_FOX * HD_FOX
RET_CHUNK = 512
PAGE = 128
ROPE_BASE = 10000.0
EPS = 1e-6
LOG2E = 1.4426950408889634
T_PAD = 16
PAGES_PER_STEP = 8
BIAS_PAGES_PER_STEP = 32
PROJ_ROW_CHUNK = 512
PROJ_WN = 256
PROJ_W_STREAMS = 2
MERGE_ROW_CHUNK = 256
Z_COLS = 12288
F_COL0 = 8192

ZB_QR, ZB_KR, ZB_VR, ZB_GR, ZB_QF, ZB_KF, ZB_VF, ZB_GF = range(8)
VMEM_LIMIT = 56 * 1024 * 1024

NT_DIMS = (((1,), (1,)), ((), ()))
TN_DIMS = (((0,), (0,)), ((), ()))


def _params(sem):
    return pltpu.CompilerParams(dimension_semantics=sem, vmem_limit_bytes=VMEM_LIMIT)


def _split3(x):
    hi = x.astype(BF16)
    r1 = x - hi.astype(F32)
    mid = r1.astype(BF16)
    lo = (r1 - mid.astype(F32)).astype(BF16)
    return hi, mid, lo


def _mod_kernel(c_ref, *rest, nw):
    w_refs, b_ref, o_ref = rest[:nw], rest[nw], rest[nw + 1]
    c = c_ref[...]
    sc = c * jax.nn.sigmoid(c)
    wn = w_refs[0].shape[1]
    for p, w_ref in enumerate(w_refs):
        cols = slice(p * wn, (p + 1) * wn)
        o_ref[:, cols] = jnp.dot(sc, w_ref[...], preferred_element_type=F32,
                                 precision=lax.Precision.HIGHEST) + b_ref[:, cols]


def _mod(c, w_ada, b_ada, tn=512, nw=1):
    n, d = c.shape
    cols = w_ada.shape[1]
    wn = tn // nw
    return pl.pallas_call(
        functools.partial(_mod_kernel, nw=nw),
        grid=(cols // tn,),
        in_specs=[pl.BlockSpec((n, d), lambda j: (0, 0)),
                  *[pl.BlockSpec((d, wn), lambda j, p=p: (0, j * nw + p)) for p in range(nw)],
                  pl.BlockSpec((1, tn), lambda j: (0, j))],
        out_specs=pl.BlockSpec((n, tn), lambda j: (0, j)),
        out_shape=jax.ShapeDtypeStruct((n, cols), F32),
        compiler_params=_params(("arbitrary",)),
        name="mod",
    )(c, *([w_ada] * nw), b_ada.reshape(1, cols))


def _wprep_kernel(lo_ref, hi_ref, olo_ref, ohi_ref, *, n_lo):
    q = pl.program_id(0)

    @pl.when(q < n_lo)
    def _():
        olo_ref[0] = lo_ref[0].astype(BF16)

    @pl.when(q >= n_lo)
    def _():
        ohi_ref[0] = hi_ref[...].astype(BF16)


def _wprep(w_full, w_tail, *, n_lo):
    d = w_full.shape[1]
    n_hi = w_tail.shape[1] // PROJ_WN
    return pl.pallas_call(
        functools.partial(_wprep_kernel, n_lo=n_lo),
        grid=(n_lo + n_hi,),
        in_specs=[pl.BlockSpec((1, d, PROJ_WN), lambda q: (0, 0, jnp.minimum(q, n_lo - 1))),
                  pl.BlockSpec((d, PROJ_WN), lambda q: (0, jnp.clip(q - n_lo, 0, n_hi - 1)))],
        out_specs=[pl.BlockSpec((1, d, PROJ_WN), lambda q: (jnp.minimum(q, n_lo - 1), 0, 0)),
                   pl.BlockSpec((1, d, PROJ_WN), lambda q: (jnp.clip(q - n_lo, 0, n_hi - 1), 0, 0))],
        out_shape=[jax.ShapeDtypeStruct((n_lo, d, PROJ_WN), BF16),
                   jax.ShapeDtypeStruct((n_hi, d, PROJ_WN), BF16)],
        compiler_params=_params(("arbitrary",)),
        name="wprep",
    )(w_full, w_tail)


def _norm_kernel(x_ref, scale_ref, shift_ref, gn_ref, wf_ref, bf_ref, h_ref, lf_ref):
    x = x_ref[...]
    ms = jnp.mean(x * x, axis=-1, keepdims=True)
    xn = x * lax.rsqrt(ms + EPS) * gn_ref[...]
    h = xn * (1.0 + scale_ref[0]) + shift_ref[0]
    hb = h.astype(BF16)
    h_ref[...] = hb
    f = jnp.dot(hb, wf_ref[...], preferred_element_type=F32)
    lf_ref[...] = jax.nn.log_sigmoid(f[:, :H_FOX] + bf_ref[...])


def _norm(x2d, scale, shift, g_norm, w_f, b_f, *, tm):
    rows, d = x2d.shape
    nrb = rows // tm
    groups, r, _ = scale.shape
    bpg = nrb // groups
    return pl.pallas_call(
        _norm_kernel,
        grid=(nrb,),
        in_specs=[
            pl.BlockSpec((tm, d), lambda i: (i, 0)),
            pl.BlockSpec((1, r, d), lambda i: (i // bpg, 0, 0)),
            pl.BlockSpec((1, r, d), lambda i: (i // bpg, 0, 0)),
            pl.BlockSpec((1, d), lambda i: (0, 0)),
            pl.BlockSpec((d, 128), lambda i: (0, 0)),
            pl.BlockSpec((1, H_FOX), lambda i: (0, 0)),
        ],
        out_specs=[pl.BlockSpec((tm, d), lambda i: (i, 0)),
                   pl.BlockSpec((tm, H_FOX), lambda i: (i, 0))],
        out_shape=[jax.ShapeDtypeStruct((rows, d), BF16),
                   jax.ShapeDtypeStruct((rows, H_FOX), F32)],
        compiler_params=_params(("arbitrary",)),
        name="norm",
    )(x2d, scale, shift, g_norm.reshape(1, d), w_f, b_f.reshape(1, H_FOX))


def _proj_kernel(h_ref, *rest, tn, rc_rows, nw, n_lo_blocks):
    lo_refs, hi_refs = rest[:nw], rest[nw:2 * nw]
    cos_ref, sin_ref, z_ref, kf_ref, vf_ref = rest[2 * nw:]
    wn = tn // nw
    j = pl.program_id(1)
    per_kb = 1024 // tn
    sec = j // per_kb
    tm = h_ref.shape[0]
    rc = min(rc_rows, tm)

    def run(epilogue, w_refs):
        for c0 in range(0, tm, rc):
            rows = slice(c0, c0 + rc)
            for p, w_ref in enumerate(w_refs):
                cols = slice(p * wn, (p + 1) * wn)
                epilogue(jnp.dot(h_ref[rows, :], w_ref[0], preferred_element_type=F32), rows, cols)

    def rope_epilogue(acc, rows, cols):
        c = cos_ref[rows, :]
        s = sin_ref[rows, :]
        sc = jnp.where(sec == ZB_KR, DK_RET ** -0.5, 1.0).astype(F32)
        for hh in range(wn // DK_RET):
            lo, mid, hi = hh * DK_RET, hh * DK_RET + DK_RET // 2, (hh + 1) * DK_RET
            x1 = acc[:, lo:mid]
            x2 = acc[:, mid:hi]
            z_ref[rows, cols.start + lo:cols.start + mid] = ((x1 * c - x2 * s) * sc).astype(BF16)
            z_ref[rows, cols.start + mid:cols.start + hi] = ((x1 * s + x2 * c) * sc).astype(BF16)

    def kf_epilogue(acc, rows, cols):
        z_ref[rows, cols] = acc.astype(BF16)
        kf_ref[rows, cols] = acc

    def vf_epilogue(acc, rows, cols):
        z_ref[rows, cols] = acc.astype(BF16)
        vf_ref[rows, cols] = acc

    is_silu = (sec == ZB_GR) | (sec == ZB_GF)
    ca = jnp.where(is_silu, 0.0, 1.0).astype(F32)
    cb = jnp.where(is_silu, 1.0, 0.0).astype(F32)

    def plain_or_silu_epilogue(acc, rows, cols):
        z_ref[rows, cols] = (acc * ca + jax.nn.sigmoid(acc) * (acc * cb)).astype(BF16)

    def sigmoid_epilogue(acc, rows, cols):
        z_ref[rows, cols] = jax.nn.sigmoid(acc).astype(BF16)

    is_rope = sec <= ZB_KR
    is_hi = j >= n_lo_blocks
    pl.when(is_rope)(lambda: run(rope_epilogue, lo_refs))
    pl.when(sec == ZB_KF)(lambda: run(kf_epilogue, lo_refs))
    pl.when(sec == ZB_VF)(lambda: run(vf_epilogue, lo_refs))
    pl.when(is_hi)(lambda: run(sigmoid_epilogue, hi_refs))
    pl.when(jnp.logical_not(is_rope | is_hi) & (sec != ZB_KF) & (sec != ZB_VF))(
        lambda: run(plain_or_silu_epilogue, lo_refs))


def _proj(h, w_lo, w_hi, cos, sin, *, tm):
    rows, d = h.shape
    nw = PROJ_W_STREAMS
    n_lo, _, wn = w_lo.shape
    n_hi = w_hi.shape[0]
    tn = nw * wn
    nrb = rows // tm
    ncb = Z_COLS // tn
    per_kb = 1024 // tn
    n_lo_blocks = n_lo // nw
    ntab = cos.shape[0] // tm
    kern = functools.partial(_proj_kernel, tn=tn, rc_rows=PROJ_ROW_CHUNK, nw=nw, n_lo_blocks=n_lo_blocks)
    kf0, vf0 = ZB_KF * per_kb, ZB_VF * per_kb
    return pl.pallas_call(
        kern,
        grid=(nrb, ncb),
        in_specs=[
            pl.BlockSpec((tm, d), lambda i, j: (i, 0)),
            *[pl.BlockSpec((1, d, wn), lambda i, j, p=p: (jnp.minimum(j * nw + p, n_lo - 1), 0, 0))
              for p in range(nw)],
            *[pl.BlockSpec((1, d, wn), lambda i, j, p=p: (jnp.clip(j * nw + p - n_lo, 0, n_hi - 1), 0, 0))
              for p in range(nw)],
            pl.BlockSpec((tm, 128), lambda i, j: (i % ntab, 0)),
            pl.BlockSpec((tm, 128), lambda i, j: (i % ntab, 0)),
        ],
        out_specs=[
            pl.BlockSpec((tm, tn), lambda i, j: (i, j)),
            pl.BlockSpec((tm, tn), lambda i, j: (i, jnp.clip(j - kf0, 0, per_kb - 1))),
            pl.BlockSpec((tm, tn), lambda i, j: (i, jnp.clip(j - vf0, 0, per_kb - 1))),
        ],
        out_shape=[
            jax.ShapeDtypeStruct((rows, Z_COLS), BF16),
            jax.ShapeDtypeStruct((rows, W_FOX), F32),
            jax.ShapeDtypeStruct((rows, W_FOX), F32),
        ],
        compiler_params=_params(("arbitrary", "arbitrary")),
        name="proj",
    )(h, *([w_lo] * nw), *([w_hi] * nw), cos, sin)


def _cumsum_kernel(lt_ref, ft_ref, *, chunk):
    s = lt_ref.shape[2]
    row = lax.broadcasted_iota(jnp.int32, (chunk, chunk), 0)
    col = lax.broadcasted_iota(jnp.int32, (chunk, chunk), 1)
    tri = jnp.where(row <= col, 1.0, 0.0).astype(BF16)
    carry = jnp.zeros((H_FOX, 1), F32)
    for c in range(s // chunk):
        lf = lt_ref[0, :, c * chunk:(c + 1) * chunk]
        terms = [t.astype(F32) for t in _split3(lf)] + [jnp.zeros_like(lf)]
        parts = jnp.dot(jnp.concatenate(terms, axis=0).astype(BF16), tri, preferred_element_type=F32)
        cs = parts[0:H_FOX] + parts[H_FOX:2 * H_FOX] + parts[2 * H_FOX:3 * H_FOX] + carry
        ft_ref[0, :, c * chunk:(c + 1) * chunk] = cs
        carry = cs[:, chunk - 1:chunk]


def _cumsum_t(lf_t, chunk=256):
    b, h, s = lf_t.shape
    return pl.pallas_call(
        functools.partial(_cumsum_kernel, chunk=chunk),
        grid=(b,),
        in_specs=[pl.BlockSpec((1, h, s), lambda i: (i, 0, 0))],
        out_specs=pl.BlockSpec((1, h, s), lambda i: (i, 0, 0)),
        out_shape=jax.ShapeDtypeStruct((b, h, s), F32),
        compiler_params=_params(("arbitrary",)),
        name="cumsum",
    )(lf_t)


def _fox_kernel(q_ref, k_ref, v_ref, sg_ref, ft_ref, o_ref, *, t, scale):
    qi = pl.program_id(2)
    c2 = scale * LOG2E
    q = q_ref[...]

    def block(kj, state, masked):
        m, l, acc = state
        rows = pl.ds(pl.multiple_of(kj * t, t), t)
        fk2 = ft_ref[0, kj] * LOG2E
        s = lax.dot_general(q, k_ref[rows, :], NT_DIMS, preferred_element_type=F32) * c2 - fk2
        if masked:
            r = lax.broadcasted_iota(jnp.int32, s.shape, 0)
            c = lax.broadcasted_iota(jnp.int32, s.shape, 1)
            s = jnp.where(r >= c, s, -jnp.inf)
        m_new = jnp.maximum(m, jnp.max(s, axis=1, keepdims=True))
        alpha = jnp.exp2(m - m_new)
        p = jnp.exp2(s - m_new)
        l = alpha * l + jnp.sum(p, axis=1, keepdims=True)
        acc = alpha * acc + jnp.dot(p.astype(BF16), v_ref[rows, :], preferred_element_type=F32)
        return m_new, l, acc

    init = (jnp.full((t, 1), -jnp.inf, F32), jnp.zeros((t, 1), F32), jnp.zeros((t, HD_FOX), F32))
    state = lax.fori_loop(0, qi, lambda kj, st: block(kj, st, False), init)
    _, l, acc = block(qi, state, True)
    o_ref[...] = ((acc / l) * sg_ref[...].astype(F32)).astype(BF16)


def _fox_prompt(z, f_t, *, batch, seq, t=512):
    nq = seq // t
    cpb = 1024 // HD_FOX
    kern = functools.partial(_fox_kernel, t=t, scale=HD_FOX ** -0.5)
    return pl.pallas_call(
        kern,
        grid=(batch, H_FOX, nq),
        in_specs=[
            pl.BlockSpec((t, HD_FOX), lambda b, h, i: (b * nq + i, ZB_QF * cpb + h)),
            pl.BlockSpec((seq, HD_FOX), lambda b, h, i: (b, ZB_KF * cpb + h)),
            pl.BlockSpec((seq, HD_FOX), lambda b, h, i: (b, ZB_VF * cpb + h)),
            pl.BlockSpec((t, HD_FOX), lambda b, h, i: (b * nq + i, ZB_GF * cpb + h)),
            pl.BlockSpec((1, nq, 1, t), lambda b, h, i: (b * H_FOX + h, 0, 0, 0)),
        ],
        out_specs=pl.BlockSpec((t, HD_FOX), lambda b, h, i: (b * nq + i, h)),
        out_shape=jax.ShapeDtypeStruct((batch * seq, W_FOX), BF16),
        compiler_params=_params(("arbitrary", "arbitrary", "arbitrary")),
        name="fox_prompt",
    )(z, z, z, z, f_t)


def _head_norm_gate(o, sg):
    on = o * lax.rsqrt(jnp.mean(o * o, axis=-1, keepdims=True) + EPS)
    return (on * sg.astype(F32)).astype(BF16)


def _ret_kernel(q_ref, k_ref, v_ref, sg_ref, dm_ref, qd_ref, kd_ref, gl_ref, a_ref, st_ref, s_scr, *, chunk):
    s_scr[...] = jnp.zeros_like(s_scr)
    dm = dm_ref[0]
    qd = qd_ref[0]
    kd = kd_ref[0]
    gl = gl_ref[0]
    nc = q_ref.shape[0] // chunk

    def body(c, carry):
        rows = pl.ds(pl.multiple_of(c * chunk, chunk), chunk)
        q = q_ref[rows, :]
        k = k_ref[rows, :]
        v = v_ref[rows, :]
        st = s_scr[...]
        sc = lax.dot_general(q, k, NT_DIMS, preferred_element_type=F32) * dm
        o = (jnp.dot(sc.astype(BF16), v, preferred_element_type=F32)
             + qd * jnp.dot(q, st.astype(BF16), preferred_element_type=F32))
        kdec = (k.astype(F32) * kd).astype(BF16)
        s_scr[...] = gl * st + lax.dot_general(kdec, v, TN_DIMS, preferred_element_type=F32)
        a_ref[rows, :] = _head_norm_gate(o, sg_ref[rows, :])
        return carry

    lax.fori_loop(0, nc, body, 0)
    st_ref[0, 0] = s_scr[...]


def _ret_tables(length, valid):
    lg = jnp.log(1.0 - 2.0 ** (-5.0 - jnp.arange(H_RET, dtype=F32)))
    idx = jnp.arange(length, dtype=F32)
    diff = idx[:, None] - idx[None, :]
    ok = (diff >= 0) & (idx[None, :] < valid)
    dm = jnp.where(ok[None], jnp.exp(lg[:, None, None] * jnp.where(ok, diff, 0.0)[None]), 0.0)
    qd = jnp.exp(lg[:, None] * (idx[None, :] + 1.0))[..., None]
    kd = jnp.where(idx[None, :] < valid, jnp.exp(lg[:, None] * (valid - 1.0 - idx[None, :])), 0.0)[..., None]
    gl = jnp.exp(lg * valid)[:, None, None]
    return dm, qd, kd, gl


def _ret_prompt(z, *, batch, seq, chunk=RET_CHUNK):
    dm, qd, kd, gl = _ret_tables(chunk, chunk)
    cpb = 1024 // DK_RET
    kern = functools.partial(_ret_kernel, chunk=chunk)
    zspec = lambda sec: pl.BlockSpec((seq, DK_RET), lambda b, h: (b, sec * cpb + h))
    tab = lambda shape: pl.BlockSpec((1,) + shape, lambda b, h: (h, 0, 0))
    return pl.pallas_call(
        kern,
        grid=(batch, H_RET),
        in_specs=[zspec(ZB_QR), zspec(ZB_KR), zspec(ZB_VR), zspec(ZB_GR),
                  tab((chunk, chunk)), tab((chunk, 1)), tab((chunk, 1)), tab((1, 1))],
        out_specs=[pl.BlockSpec((seq, DV_RET), lambda b, h: (b, h)),
                   pl.BlockSpec((1, 1, DK_RET, DV_RET), lambda b, h: (b, h, 0, 0))],
        out_shape=[jax.ShapeDtypeStruct((batch * seq, W_RET), BF16),
                   jax.ShapeDtypeStruct((batch, H_RET, DK_RET, DV_RET), F32)],
        scratch_shapes=[pltpu.VMEM((DK_RET, DV_RET), F32)],
        compiler_params=_params(("arbitrary", "arbitrary")),
        name="ret_prompt",
    )(z, z, z, z, dm, qd, kd, gl)


def _ret_sample_kernel(q_ref, k_ref, v_ref, sg_ref, st_ref, dm_ref, qd_ref, kd_ref, gl_ref, a_ref, nst_ref):
    tp = q_ref.shape[1]
    pad = jnp.zeros((PAGE - tp, DK_RET), BF16)
    for h in range(H_RET):
        cols = slice(h * DK_RET, (h + 1) * DK_RET)
        q = q_ref[0, :, cols]
        k = k_ref[0, :, cols]
        v = v_ref[0, :, cols]
        kp = jnp.concatenate([k, pad], axis=0)
        vp = jnp.concatenate([v, pad], axis=0)
        kdp = jnp.concatenate([(k.astype(F32) * kd_ref[h]).astype(BF16), pad], axis=0)
        st = st_ref[0, h]
        sc = lax.dot_general(q, kp, NT_DIMS, preferred_element_type=F32) * dm_ref[h]
        o = (jnp.dot(sc.astype(BF16), vp, preferred_element_type=F32)
             + qd_ref[h] * jnp.dot(q, st.astype(BF16), preferred_element_type=F32))
        nst_ref[0, h] = gl_ref[h] * st + lax.dot_general(kdp, vp, TN_DIMS, preferred_element_type=F32)
        a_ref[0, :, cols] = _head_norm_gate(o, sg_ref[0, :, cols])


def _ret_sample(z3, state, *, valid):
    b, tp, _ = z3.shape
    dm, qd, kd, gl = _ret_tables(tp, valid)
    dm = jnp.pad(dm, ((0, 0), (0, 0), (0, PAGE - tp)))
    zspec = lambda sec: pl.BlockSpec((1, tp, 1024), lambda i: (i, 0, sec))
    full = lambda a: pl.BlockSpec(a.shape, lambda i: (0,) * a.ndim)
    st_spec = pl.BlockSpec((1, H_RET, DK_RET, DV_RET), lambda i: (i, 0, 0, 0))
    return pl.pallas_call(
        _ret_sample_kernel,
        grid=(b,),
        in_specs=[zspec(ZB_QR), zspec(ZB_KR), zspec(ZB_VR), zspec(ZB_GR), st_spec,
                  full(dm), full(qd), full(kd), full(gl)],
        out_specs=[pl.BlockSpec((1, tp, W_RET), lambda i: (i, 0, 0)), st_spec],
        out_shape=[jax.ShapeDtypeStruct((b, tp, W_RET), BF16),
                   jax.ShapeDtypeStruct(state.shape, F32)],
        compiler_params=_params(("arbitrary",)),
        name="ret_sample",
    )(z3, z3, z3, z3, state, dm, qd, kd, gl)


def _fox_bias_kernel(pt_ref, *refs, pps):
    del pt_ref
    lf_refs = refs[:pps]
    o_ref, carry_scr = refs[pps], refs[pps + 1]

    @pl.when(pl.program_id(1) == 0)
    def _():
        carry_scr[...] = jnp.zeros_like(carry_scr)

    lf = jnp.concatenate([r[0] for r in lf_refs], axis=0)
    nl = lf.shape[0]
    jj = lax.broadcasted_iota(jnp.int32, (PAGE, 2 * PAGE), 0)
    ss = lax.broadcasted_iota(jnp.int32, (PAGE, 2 * PAGE), 1)
    same_head = (jj % H_FOX) == (ss % H_FOX)
    w = jnp.where(same_head & ((ss >= PAGE) | (jj // H_FOX > ss // H_FOX)), 1.0, 0.0).astype(BF16)
    parts = jnp.dot(jnp.concatenate(_split3(lf), axis=0), w, preferred_element_type=F32)
    both = parts[0:nl] + parts[nl:2 * nl] + parts[2 * nl:3 * nl]
    within = both[:, :PAGE]
    tot = both[:, PAGE:]
    kdim = max(nl, PAGE)
    ri = lax.broadcasted_iota(jnp.int32, (nl, kdim), 0)
    ci = lax.broadcasted_iota(jnp.int32, (nl, kdim), 1)
    later = jnp.where((ci > ri) & (ci < nl), 1.0, 0.0).astype(BF16)
    t3 = jnp.concatenate(_split3(tot), axis=1)
    if nl < kdim:
        t3 = jnp.concatenate([t3, jnp.zeros((kdim - nl, 3 * PAGE), BF16)], axis=0)
    cross3 = jnp.dot(later, t3, preferred_element_type=F32)
    cross = cross3[:, :PAGE] + cross3[:, PAGE:2 * PAGE] + cross3[:, 2 * PAGE:]
    carry = carry_scr[...]
    r = within + cross + carry
    carry_scr[...] = carry + jnp.sum(tot, axis=0, keepdims=True)
    rows_per_page = PAGE * H_FOX // PAGE
    for pg in range(pps):
        for c in range(rows_per_page):
            o_ref[0, pg, :, c * PAGE:(c + 1) * PAGE] = r[pg * rows_per_page + c:pg * rows_per_page + c + 1, :]


def _fox_bias(cache_lf, page_table, *, pps):
    b, n_pages = page_table.shape
    steps = n_pages // pps
    rpp = cache_lf.shape[1]

    def page_map(r):
        return lambda i, p, pt: (pt[i * n_pages + (steps - 1 - p) * pps + r], 0, 0)

    grid_spec = pltpu.PrefetchScalarGridSpec(
        num_scalar_prefetch=1,
        grid=(b, steps),
        in_specs=[pl.BlockSpec((1, rpp, PAGE), page_map(r)) for r in range(pps)],
        out_specs=pl.BlockSpec((1, pps, 1, rpp * PAGE), lambda i, p, pt: (i, steps - 1 - p, 0, 0)),
        scratch_shapes=[pltpu.VMEM((1, PAGE), F32)],
    )
    return pl.pallas_call(
        functools.partial(_fox_bias_kernel, pps=pps),
        grid_spec=grid_spec,
        out_shape=jax.ShapeDtypeStruct((b, n_pages, 1, rpp * PAGE), F32),
        compiler_params=_params(("arbitrary", "arbitrary")),
        name="fox_bias",
    )(page_table.reshape(-1), *([cache_lf] * pps))


def _heads_to_rows(x_row):
    return jnp.concatenate([x_row[:, h * HD_FOX:(h + 1) * HD_FOX] for h in range(H_FOX)], axis=0)


def _fox_sample_kernel(pt_ref, *refs, n_new, scale, pps):
    del pt_ref
    k_refs = refs[0:pps]
    v_refs = refs[pps:2 * pps]
    r_ref, q_ref, kn_ref, vn_ref, sg_ref, lfn_ref, o_ref = refs[2 * pps:2 * pps + 7]
    q_scr, m_scr, l_scr, acc_scr, g_scr = refs[2 * pps + 7:]
    p_id = pl.program_id(1)
    n_rows = n_new * H_FOX
    n_keys = PAGE * H_FOX

    @pl.when(p_id == 0)
    def _():
        q = q_ref[0].astype(F32)
        q_scr[...] = jnp.concatenate([_heads_to_rows(q[t:t + 1, :]) for t in range(n_new)], axis=0).astype(BF16)
        m_scr[...] = jnp.full_like(m_scr, -jnp.inf)
        l_scr[...] = jnp.zeros_like(l_scr)
        acc_scr[...] = jnp.zeros_like(acc_scr)
        lfn = lfn_ref[0]
        r8 = lax.broadcasted_iota(jnp.int32, (H_FOX, H_FOX), 0)
        c8 = lax.broadcasted_iota(jnp.int32, (H_FOX, H_FOX), 1)
        g = jnp.zeros((1, H_FOX), F32)
        cols = []
        for t in range(n_new):
            g = g + lfn[t:t + 1, :]
            cols.append(jnp.sum(jnp.where(r8 == c8, jnp.broadcast_to(g, (H_FOX, H_FOX)), 0.0), axis=1, keepdims=True))
        g_scr[...] = jnp.concatenate(cols, axis=0)

    qm = q_scr[...]
    gcol = g_scr[...]
    row = lax.broadcasted_iota(jnp.int32, (n_rows, n_keys), 0)
    lane = lax.broadcasted_iota(jnp.int32, (n_rows, n_keys), 1)
    own_head = (lane % H_FOX) == (row % H_FOX)

    logits = []
    for r in range(pps):
        kp = k_refs[r][0].reshape(n_keys, HD_FOX).astype(BF16)
        s = lax.dot_general(qm, kp, NT_DIMS, preferred_element_type=F32)
        logits.append(jnp.where(own_head, s * scale + gcol + r_ref[0, r], -jnp.inf))
    m_old = m_scr[...]
    m_new = m_old
    for lg in logits:
        m_new = jnp.maximum(m_new, jnp.max(lg, axis=1, keepdims=True))
    alpha = jnp.exp(m_old - m_new)
    l = alpha * l_scr[...]
    acc = alpha * acc_scr[...]
    for r, lg in enumerate(logits):
        p = jnp.exp(lg - m_new)
        l = l + jnp.sum(p, axis=1, keepdims=True)
        vp = v_refs[r][0].reshape(n_keys, HD_FOX).astype(BF16)
        acc = acc + jnp.dot(p.astype(BF16), vp, preferred_element_type=F32)
    m_scr[...] = m_new
    l_scr[...] = l
    acc_scr[...] = acc

    @pl.when(p_id == pl.num_programs(1) - 1)
    def _():
        qf = q_scr[...].astype(F32)
        kn = kn_ref[0].astype(F32)
        vn = vn_ref[0].astype(F32)
        g_col = g_scr[...]
        tok = lax.broadcasted_iota(jnp.int32, (n_rows, 1), 0) // H_FOX
        tile = lambda x: jnp.concatenate([x] * n_new, axis=0)
        lgs = []
        for t in range(n_new):
            sn = jnp.sum(qf * tile(_heads_to_rows(kn[t:t + 1, :])), axis=1, keepdims=True) * scale
            g_t = tile(g_col[t * H_FOX:(t + 1) * H_FOX, :])
            lgs.append(jnp.where(tok >= t, sn + g_col - g_t, -jnp.inf))
        m0 = m_scr[...]
        m1 = m0
        for lg in lgs:
            m1 = jnp.maximum(m1, lg)
        a1 = jnp.exp(m0 - m1)
        l1 = a1 * l_scr[...]
        acc1 = a1 * acc_scr[...]
        for t, lg in enumerate(lgs):
            pt = jnp.exp(lg - m1)
            l1 = l1 + pt
            acc1 = acc1 + pt * tile(_heads_to_rows(vn[t:t + 1, :]))
        o = acc1 / l1
        outs = [jnp.concatenate([o[t * H_FOX + h:t * H_FOX + h + 1, :] for h in range(H_FOX)], axis=1)
                for t in range(n_new)]
        outs.append(jnp.zeros((o_ref.shape[1] - n_new, W_FOX), F32))
        o_ref[0] = (jnp.concatenate(outs, axis=0) * sg_ref[0].astype(F32)).astype(BF16)


def _fox_sample(z3, lf_new, cache_k, cache_v, bias_rows, page_table, *, n_new, pps):
    b, tp, _ = z3.shape
    n_pages = page_table.shape[1]
    steps = n_pages // pps
    n_rows = n_new * H_FOX

    def page_map(r):
        return lambda i, p, pt: (pt[i * n_pages + (steps - 1 - p) * pps + r], 0, 0, 0)

    kv_specs = [pl.BlockSpec((1, PAGE, H_FOX, HD_FOX), page_map(r)) for r in range(pps)]
    zspec = lambda sec: pl.BlockSpec((1, tp, 1024), lambda i, p, pt: (i, 0, sec))
    kern = functools.partial(_fox_sample_kernel, n_new=n_new, scale=HD_FOX ** -0.5, pps=pps)
    grid_spec = pltpu.PrefetchScalarGridSpec(
        num_scalar_prefetch=1,
        grid=(b, steps),
        in_specs=kv_specs + kv_specs + [
            pl.BlockSpec((1, pps, 1, PAGE * H_FOX), lambda i, p, pt: (i, steps - 1 - p, 0, 0)),
            zspec(ZB_QF), zspec(ZB_KF), zspec(ZB_VF), zspec(ZB_GF),
            pl.BlockSpec((1, tp, H_FOX), lambda i, p, pt: (i, 0, 0)),
        ],
        out_specs=pl.BlockSpec((1, tp, W_FOX), lambda i, p, pt: (i, 0, 0)),
        scratch_shapes=[
            pltpu.VMEM((n_rows, HD_FOX), BF16),
            pltpu.VMEM((n_rows, 1), F32),
            pltpu.VMEM((n_rows, 1), F32),
            pltpu.VMEM((n_rows, HD_FOX), F32),
            pltpu.VMEM((n_rows, 1), F32),
        ],
    )
    return pl.pallas_call(
        kern,
        grid_spec=grid_spec,
        out_shape=jax.ShapeDtypeStruct((b, tp, W_FOX), BF16),
        compiler_params=_params(("arbitrary", "arbitrary")),
        name="fox_sample",
    )(page_table.reshape(-1), *([cache_k] * pps), *([cache_v] * pps), bias_rows, z3, z3, z3, z3, lf_new)


def _merge_kernel(ar_ref, af_ref, mr_ref, mf_ref, x_ref, gate_ref, wb_ref, wo_ref, gf_ref, y_ref, *, rc):
    tm = x_ref.shape[0]
    for c0 in range(0, tm, rc):
        rows = slice(c0, c0 + rc)
        p_r = jnp.dot(ar_ref[rows, :], wb_ref[0:W_RET, :], preferred_element_type=F32)
        p_f = jnp.dot(af_ref[rows, :], wb_ref[W_RET:W_RET + W_FOX, :], preferred_element_type=F32)
        merged = mr_ref[rows, :].astype(F32) * p_r + mf_ref[rows, :].astype(F32) * p_f
        gate = gate_ref[0] if gate_ref.shape[1] == 1 else gate_ref[0, rows, :]
        out = x_ref[rows, :] + gate * jnp.dot(merged.astype(BF16), wo_ref[...], preferred_element_type=F32)
        y = out * lax.rsqrt(jnp.mean(out * out, axis=-1, keepdims=True) + EPS)
        y_ref[rows, :] = y * gf_ref[...]


def _merge(a_r, a_f, z, x2d, gate, w_branch, w_out, g_final, *, tm):
    rows, d = x2d.shape
    nrb = rows // tm
    groups, r, _ = gate.shape
    bpg = nrb // groups
    return pl.pallas_call(
        functools.partial(_merge_kernel, rc=min(MERGE_ROW_CHUNK, tm)),
        grid=(nrb,),
        in_specs=[
            pl.BlockSpec((tm, W_RET), lambda i: (i, 0)),
            pl.BlockSpec((tm, W_FOX), lambda i: (i, 0)),
            pl.BlockSpec((tm, d), lambda i: (i, 4)),
            pl.BlockSpec((tm, d), lambda i: (i, 5)),
            pl.BlockSpec((tm, d), lambda i: (i, 0)),
            pl.BlockSpec((1, r, d), lambda i: (i // bpg, 0, 0)),
            pl.BlockSpec(w_branch.shape, lambda i: (0, 0)),
            pl.BlockSpec(w_out.shape, lambda i: (0, 0)),
            pl.BlockSpec((1, d), lambda i: (0, 0)),
        ],
        out_specs=pl.BlockSpec((tm, d), lambda i: (i, 0)),
        out_shape=jax.ShapeDtypeStruct((rows, d), F32),
        compiler_params=_params(("arbitrary",)),
        name="merge",
    )(a_r, a_f, z, z, x2d, gate, w_branch, w_out, g_final.reshape(1, d))


def _rope_tables(pos):
    inv = 1.0 / (ROPE_BASE ** (jnp.arange(0, DK_RET, 2, dtype=F32) / DK_RET))
    ang = pos.astype(F32)[:, None] * inv[None, :]
    return jnp.cos(ang), jnp.sin(ang)


def kernel(x_prompt, x_sample, c_prompt, c_sample, cache_k, cache_v, cache_logf, state_ret, page_table,
           w_in, b_fgt, g_norm, w_ada, b_ada, w_branch, w_out, g_final):
    depth = w_in.shape[0]
    assert depth == 1, "the layer loop is written for the single-layer configuration"
    batch, seq, d = x_prompt.shape
    dec_batch, dec_seq, _ = x_sample.shape
    n_pages = page_table.shape[1]
    past_len = n_pages * PAGE
    l = 0

    w_l = w_in[l]
    w_lo, w_hi = _wprep(w_in, w_l[:, F_COL0 + H_FOX:], n_lo=F_COL0 // PROJ_WN)
    w_f = jnp.pad(w_l[:, F_COL0:F_COL0 + H_FOX], ((0, 0), (0, 128 - H_FOX))).astype(BF16)
    wb = w_branch[l].astype(BF16)
    wo = w_out[l].astype(BF16)

    mod = _mod(jnp.concatenate([c_prompt, c_sample], axis=0), w_ada[l], b_ada[l])
    shift, scale, gate = jnp.split(mod, 3, axis=-1)

    rows_p = batch * seq
    xp2 = x_prompt.reshape(rows_p, d)
    cos_p, sin_p = _rope_tables(jnp.arange(seq))
    h_p, lf_p = _norm(xp2, scale[:batch, None, :], shift[:batch, None, :], g_norm[l], w_f, b_fgt[l],
                      tm=min(512, seq))
    z_p, kf_p, vf_p = _proj(h_p, w_lo, w_hi, cos_p, sin_p, tm=min(2048, seq))
    t_fox = min(1024, seq)
    f_t = _cumsum_t(lf_p.reshape(batch, seq, H_FOX).transpose(0, 2, 1), chunk=min(256, seq))
    a_f_p = _fox_prompt(z_p, f_t.reshape(batch * H_FOX, seq // t_fox, 1, t_fox),
                        batch=batch, seq=seq, t=t_fox)
    a_r_p, st_p = _ret_prompt(z_p, batch=batch, seq=seq)
    y_p = _merge(a_r_p, a_f_p, z_p, xp2, gate[:batch, None, :], wb, wo, g_final, tm=min(256, rows_p))

    rows_s = dec_batch * T_PAD
    xs2 = jnp.pad(x_sample, ((0, 0), (0, T_PAD - dec_seq), (0, 0))).reshape(rows_s, d)
    pos_s = past_len + (jnp.arange(rows_s) % T_PAD)
    cos_s, sin_s = _rope_tables(pos_s)
    tm_s = min(256, rows_s)
    rep = lambda a: jnp.repeat(a[batch:], T_PAD, axis=0).reshape(rows_s // tm_s, tm_s, d)
    h_s, lf_s = _norm(xs2, rep(scale), rep(shift), g_norm[l], w_f, b_fgt[l], tm=tm_s)
    z_s, kf_s, vf_s = _proj(h_s, w_lo, w_hi, cos_s, sin_s, tm=tm_s)
    z_s3 = z_s.reshape(dec_batch, T_PAD, Z_COLS)
    a_r_s, st_s = _ret_sample(z_s3, state_ret[l].astype(F32), valid=dec_seq)
    n_pool = cache_k.shape[1]
    pps = min(PAGES_PER_STEP, n_pages)
    bias = _fox_bias(cache_logf[l].reshape(n_pool, H_FOX, PAGE), page_table,
                     pps=min(BIAS_PAGES_PER_STEP, n_pages))
    a_f_s = _fox_sample(z_s3, lf_s.reshape(dec_batch, T_PAD, H_FOX), cache_k[l], cache_v[l],
                        bias, page_table, n_new=dec_seq, pps=pps)
    y_s = _merge(a_r_s.reshape(rows_s, W_RET), a_f_s.reshape(rows_s, W_FOX), z_s, xs2, rep(gate),
                 wb, wo, g_final, tm=tm_s)

    take = lambda a, tail: a.reshape((dec_batch, T_PAD) + tail)[:, :dec_seq]
    return (y_p.reshape(batch, seq, d),
            take(y_s, (d,)),
            kf_p.reshape(1, batch, seq, H_FOX, HD_FOX),
            vf_p.reshape(1, batch, seq, H_FOX, HD_FOX),
            lf_p.reshape(1, batch, seq, H_FOX),
            st_p[None],
            take(kf_s, (H_FOX, HD_FOX))[None],
            take(vf_s, (H_FOX, HD_FOX))[None],
            take(lf_s, (H_FOX,))[None],
            st_s[None])
```

```python
import functools

import jax
import jax.numpy as jnp
import numpy as np
from jax import lax
from jax.experimental import pallas as pl
from jax.experimental.pallas import tpu as pltpu

F32 = jnp.float32
BF16 = jnp.bfloat16

D_MODEL = 2048
H_RET, DK_RET, DV_RET = 4, 256, 256
H_FOX, HD_FOX = 8, 128
W_RET = H_RET * DV_RET
W_FOX = H_FOX * HD_FOX
RET_CHUNK = 512
PAGE = 128
ROPE_BASE = 10000.0
EPS = 1e-6
LOG2E = 1.4426950408889634
T_PAD = 16
PAGES_PER_STEP = 8
BIAS_PAGES_PER_STEP = 32
PROJ_ROW_CHUNK = 512
PROJ_WN = 256
PROJ_W_STREAMS = 2
MERGE_ROW_CHUNK = 256
Z_COLS = 12288
F_COL0 = 8192

ZB_QR, ZB_KR, ZB_VR, ZB_GR, ZB_QF, ZB_KF, ZB_VF, ZB_GF = range(8)
VMEM_LIMIT = 56 * 1024 * 1024

NT_DIMS = (((1,), (1,)), ((), ()))
TN_DIMS = (((0,), (0,)), ((), ()))


def _params(sem):
    return pltpu.CompilerParams(dimension_semantics=sem, vmem_limit_bytes=VMEM_LIMIT)


def _split3(x):
    hi = x.astype(BF16)
    r1 = x - hi.astype(F32)
    mid = r1.astype(BF16)
    lo = (r1 - mid.astype(F32)).astype(BF16)
    return hi, mid, lo


def _mod_kernel(c_ref, *rest, nw):
    w_refs, b_ref, o_ref = rest[:nw], rest[nw], rest[nw + 1]
    c = c_ref[...]
    sc = c * jax.nn.sigmoid(c)
    wn = w_refs[0].shape[1]
    for p, w_ref in enumerate(w_refs):
        cols = slice(p * wn, (p + 1) * wn)
        o_ref[:, cols] = jnp.dot(sc, w_ref[...], preferred_element_type=F32,
                                 precision=lax.Precision.HIGHEST) + b_ref[:, cols]


def _mod(c, w_ada, b_ada, tn=512, nw=1):
    n, d = c.shape
    cols = w_ada.shape[1]
    wn = tn // nw
    return pl.pallas_call(
        functools.partial(_mod_kernel, nw=nw),
        grid=(cols // tn,),
        in_specs=[pl.BlockSpec((n, d), lambda j: (0, 0)),
                  *[pl.BlockSpec((d, wn), lambda j, p=p: (0, j * nw + p)) for p in range(nw)],
                  pl.BlockSpec((1, tn), lambda j: (0, j))],
        out_specs=pl.BlockSpec((n, tn), lambda j: (0, j)),
        out_shape=jax.ShapeDtypeStruct((n, cols), F32),
        compiler_params=_params(("arbitrary",)),
        name="mod",
    )(c, *([w_ada] * nw), b_ada.reshape(1, cols))


def _norm_kernel(x_ref, scale_ref, shift_ref, gn_ref, wf_ref, bf_ref, h_ref, lf_ref):
    x = x_ref[...]
    ms = jnp.mean(x * x, axis=-1, keepdims=True)
    xn = x * lax.rsqrt(ms + EPS) * gn_ref[...]
    h = xn * (1.0 + scale_ref[0]) + shift_ref[0]
    hb = h.astype(BF16)
    h_ref[...] = hb
    f = jnp.dot(hb, wf_ref[...], preferred_element_type=F32)
    lf_ref[...] = jax.nn.log_sigmoid(f[:, :H_FOX] + bf_ref[...])


def _norm(x2d, scale, shift, g_norm, w_f, b_f, *, tm):
    rows, d = x2d.shape
    nrb = rows // tm
    groups, r, _ = scale.shape
    bpg = nrb // groups
    return pl.pallas_call(
        _norm_kernel,
        grid=(nrb,),
        in_specs=[
            pl.BlockSpec((tm, d), lambda i: (i, 0)),
            pl.BlockSpec((1, r, d), lambda i: (i // bpg, 0, 0)),
            pl.BlockSpec((1, r, d), lambda i: (i // bpg, 0, 0)),
            pl.BlockSpec((1, d), lambda i: (0, 0)),
            pl.BlockSpec((d, 128), lambda i: (0, 0)),
            pl.BlockSpec((1, H_FOX), lambda i: (0, 0)),
        ],
        out_specs=[pl.BlockSpec((tm, d), lambda i: (i, 0)),
                   pl.BlockSpec((tm, H_FOX), lambda i: (i, 0))],
        out_shape=[jax.ShapeDtypeStruct((rows, d), BF16),
                   jax.ShapeDtypeStruct((rows, H_FOX), F32)],
        compiler_params=_params(("arbitrary",)),
        name="norm",
    )(x2d, scale, shift, g_norm.reshape(1, d), w_f, b_f.reshape(1, H_FOX))


def _proj_kernel(h_ref, *rest, tn, rc_rows, nw, n_lo_blocks):
    lo_refs, hi_refs = rest[:nw], rest[nw:2 * nw]
    cos_ref, sin_ref, z_ref, kf_ref, vf_ref = rest[2 * nw:]
    wn = tn // nw
    j = pl.program_id(1)
    per_kb = 1024 // tn
    sec = j // per_kb
    tm = h_ref.shape[0]
    rc = min(rc_rows, tm)

    def run(epilogue, w_refs):
        for c0 in range(0, tm, rc):
            rows = slice(c0, c0 + rc)
            for p, w_ref in enumerate(w_refs):
                cols = slice(p * wn, (p + 1) * wn)
                epilogue(jnp.dot(h_ref[rows, :], w_ref[0], preferred_element_type=F32), rows, cols)

    def rope_epilogue(acc, rows, cols):
        c = cos_ref[rows, :]
        s = sin_ref[rows, :]
        sc = jnp.where(sec == ZB_KR, DK_RET ** -0.5, 1.0).astype(F32)
        for hh in range(wn // DK_RET):
            lo, mid, hi = hh * DK_RET, hh * DK_RET + DK_RET // 2, (hh + 1) * DK_RET
            x1 = acc[:, lo:mid]
            x2 = acc[:, mid:hi]
            z_ref[rows, cols.start + lo:cols.start + mid] = ((x1 * c - x2 * s) * sc).astype(BF16)
            z_ref[rows, cols.start + mid:cols.start + hi] = ((x1 * s + x2 * c) * sc).astype(BF16)

    def kf_epilogue(acc, rows, cols):
        z_ref[rows, cols] = acc.astype(BF16)
        kf_ref[rows, cols] = acc

    def vf_epilogue(acc, rows, cols):
        z_ref[rows, cols] = acc.astype(BF16)
        vf_ref[rows, cols] = acc

    is_silu = (sec == ZB_GR) | (sec == ZB_GF)
    ca = jnp.where(is_silu, 0.0, 1.0).astype(F32)
    cb = jnp.where(is_silu, 1.0, 0.0).astype(F32)

    def plain_or_silu_epilogue(acc, rows, cols):
        z_ref[rows, cols] = (acc * ca + jax.nn.sigmoid(acc) * (acc * cb)).astype(BF16)

    def sigmoid_epilogue(acc, rows, cols):
        z_ref[rows, cols] = jax.nn.sigmoid(acc).astype(BF16)

    is_rope = sec <= ZB_KR
    is_hi = j >= n_lo_blocks
    pl.when(is_rope)(lambda: run(rope_epilogue, lo_refs))
    pl.when(sec == ZB_KF)(lambda: run(kf_epilogue, lo_refs))
    pl.when(sec == ZB_VF)(lambda: run(vf_epilogue, lo_refs))
    pl.when(is_hi)(lambda: run(sigmoid_epilogue, hi_refs))
    pl.when(jnp.logical_not(is_rope | is_hi) & (sec != ZB_KF) & (sec != ZB_VF))(
        lambda: run(plain_or_silu_epilogue, lo_refs))


def _proj(h, w_lo, w_hi, cos, sin, *, tm):
    rows, d = h.shape
    nw = PROJ_W_STREAMS
    n_lo, _, wn = w_lo.shape
    n_hi = w_hi.shape[0]
    tn = nw * wn
    nrb = rows // tm
    ncb = Z_COLS // tn
    per_kb = 1024 // tn
    n_lo_blocks = n_lo // nw
    ntab = cos.shape[0] // tm
    kern = functools.partial(_proj_kernel, tn=tn, rc_rows=PROJ_ROW_CHUNK, nw=nw, n_lo_blocks=n_lo_blocks)
    kf0, vf0 = ZB_KF * per_kb, ZB_VF * per_kb
    return pl.pallas_call(
        kern,
        grid=(nrb, ncb),
        in_specs=[
            pl.BlockSpec((tm, d), lambda i, j: (i, 0)),
            *[pl.BlockSpec((1, d, wn), lambda i, j, p=p: (jnp.minimum(j * nw + p, n_lo - 1), 0, 0))
              for p in range(nw)],
            *[pl.BlockSpec((1, d, wn), lambda i, j, p=p: (jnp.clip(j * nw + p - n_lo, 0, n_hi - 1), 0, 0))
              for p in range(nw)],
            pl.BlockSpec((tm, 128), lambda i, j: (i % ntab, 0)),
            pl.BlockSpec((tm, 128), lambda i, j: (i % ntab, 0)),
        ],
        out_specs=[
            pl.BlockSpec((tm, tn), lambda i, j: (i, j)),
            pl.BlockSpec((tm, tn), lambda i, j: (i, jnp.clip(j - kf0, 0, per_kb - 1))),
            pl.BlockSpec((tm, tn), lambda i, j: (i, jnp.clip(j - vf0, 0, per_kb - 1))),
        ],
        out_shape=[
            jax.ShapeDtypeStruct((rows, Z_COLS), BF16),
            jax.ShapeDtypeStruct((rows, W_FOX), F32),
            jax.ShapeDtypeStruct((rows, W_FOX), F32),
        ],
        compiler_params=_params(("arbitrary", "arbitrary")),
        name="proj",
    )(h, *([w_lo] * nw), *([w_hi] * nw), cos, sin)


def _cumsum_kernel(lt_ref, ft_ref, *, chunk):
    s = lt_ref.shape[2]
    row = lax.broadcasted_iota(jnp.int32, (chunk, chunk), 0)
    col = lax.broadcasted_iota(jnp.int32, (chunk, chunk), 1)
    tri = jnp.where(row <= col, 1.0, 0.0).astype(BF16)
    carry = jnp.zeros((H_FOX, 1), F32)
    for c in range(s // chunk):
        lf = lt_ref[0, :, c * chunk:(c + 1) * chunk]
        terms = [t.astype(F32) for t in _split3(lf)] + [jnp.zeros_like(lf)]
        parts = jnp.dot(jnp.concatenate(terms, axis=0).astype(BF16), tri, preferred_element_type=F32)
        cs = parts[0:H_FOX] + parts[H_FOX:2 * H_FOX] + parts[2 * H_FOX:3 * H_FOX] + carry
        ft_ref[0, :, c * chunk:(c + 1) * chunk] = cs
        carry = cs[:, chunk - 1:chunk]


def _cumsum_t(lf_t, chunk=256):
    b, h, s = lf_t.shape
    return pl.pallas_call(
        functools.partial(_cumsum_kernel, chunk=chunk),
        grid=(b,),
        in_specs=[pl.BlockSpec((1, h, s), lambda i: (i, 0, 0))],
        out_specs=pl.BlockSpec((1, h, s), lambda i: (i, 0, 0)),
        out_shape=jax.ShapeDtypeStruct((b, h, s), F32),
        compiler_params=_params(("arbitrary",)),
        name="cumsum",
    )(lf_t)


def _fox_kernel(q_ref, k_ref, v_ref, sg_ref, ft_ref, o_ref, *, t, scale):
    qi = pl.program_id(2)
    c2 = scale * LOG2E
    q = q_ref[...]

    def block(kj, state, masked):
        m, l, acc = state
        rows = pl.ds(pl.multiple_of(kj * t, t), t)
        fk2 = ft_ref[0, kj] * LOG2E
        s = lax.dot_general(q, k_ref[rows, :], NT_DIMS, preferred_element_type=F32) * c2 - fk2
        if masked:
            r = lax.broadcasted_iota(jnp.int32, s.shape, 0)
            c = lax.broadcasted_iota(jnp.int32, s.shape, 1)
            s = jnp.where(r >= c, s, -jnp.inf)
        m_new = jnp.maximum(m, jnp.max(s, axis=1, keepdims=True))
        alpha = jnp.exp2(m - m_new)
        p = jnp.exp2(s - m_new)
        l = alpha * l + jnp.sum(p, axis=1, keepdims=True)
        acc = alpha * acc + jnp.dot(p.astype(BF16), v_ref[rows, :], preferred_element_type=F32)
        return m_new, l, acc

    init = (jnp.full((t, 1), -jnp.inf, F32), jnp.zeros((t, 1), F32), jnp.zeros((t, HD_FOX), F32))
    state = lax.fori_loop(0, qi, lambda kj, st: block(kj, st, False), init)
    _, l, acc = block(qi, state, True)
    o_ref[...] = ((acc / l) * sg_ref[...].astype(F32)).astype(BF16)


def _fox_prompt(z, f_t, *, batch, seq, t=512):
    nq = seq // t
    cpb = 1024 // HD_FOX
    kern = functools.partial(_fox_kernel, t=t, scale=HD_FOX ** -0.5)
    return pl.pallas_call(
        kern,
        grid=(batch, H_FOX, nq),
        in_specs=[
            pl.BlockSpec((t, HD_FOX), lambda b, h, i: (b * nq + i, ZB_QF * cpb + h)),
            pl.BlockSpec((seq, HD_FOX), lambda b, h, i: (b, ZB_KF * cpb + h)),
            pl.BlockSpec((seq, HD_FOX), lambda b, h, i: (b, ZB_VF * cpb + h)),
            pl.BlockSpec((t, HD_FOX), lambda b, h, i: (b * nq + i, ZB_GF * cpb + h)),
            pl.BlockSpec((1, nq, 1, t), lambda b, h, i: (b * H_FOX + h, 0, 0, 0)),
        ],
        out_specs=pl.BlockSpec((t, HD_FOX), lambda b, h, i: (b * nq + i, h)),
        out_shape=jax.ShapeDtypeStruct((batch * seq, W_FOX), BF16),
        compiler_params=_params(("arbitrary", "arbitrary", "arbitrary")),
        name="fox_prompt",
    )(z, z, z, z, f_t)


def _head_norm_gate(o, sg):
    on = o * lax.rsqrt(jnp.mean(o * o, axis=-1, keepdims=True) + EPS)
    return (on * sg.astype(F32)).astype(BF16)


def _ret_kernel(q_ref, k_ref, v_ref, sg_ref, dm_ref, qd_ref, kd_ref, gl_ref, a_ref, st_ref, s_scr, *, chunk):
    s_scr[...] = jnp.zeros_like(s_scr)
    dm = dm_ref[0]
    qd = qd_ref[0]
    kd = kd_ref[0]
    gl = gl_ref[0]
    nc = q_ref.shape[0] // chunk

    def body(c, carry):
        rows = pl.ds(pl.multiple_of(c * chunk, chunk), chunk)
        q = q_ref[rows, :]
        k = k_ref[rows, :]
        v = v_ref[rows, :]
        st = s_scr[...]
        sc = lax.dot_general(q, k, NT_DIMS, preferred_element_type=F32) * dm
        o = (jnp.dot(sc.astype(BF16), v, preferred_element_type=F32)
             + qd * jnp.dot(q, st.astype(BF16), preferred_element_type=F32))
        kdec = (k.astype(F32) * kd).astype(BF16)
        s_scr[...] = gl * st + lax.dot_general(kdec, v, TN_DIMS, preferred_element_type=F32)
        a_ref[rows, :] = _head_norm_gate(o, sg_ref[rows, :])
        return carry

    lax.fori_loop(0, nc, body, 0)
    st_ref[0, 0] = s_scr[...]


def _ret_tables(length, valid):
    lg = jnp.log(1.0 - 2.0 ** (-5.0 - jnp.arange(H_RET, dtype=F32)))
    idx = jnp.arange(length, dtype=F32)
    diff = idx[:, None] - idx[None, :]
    ok = (diff >= 0) & (idx[None, :] < valid)
    dm = jnp.where(ok[None], jnp.exp(lg[:, None, None] * jnp.where(ok, diff, 0.0)[None]), 0.0)
    qd = jnp.exp(lg[:, None] * (idx[None, :] + 1.0))[..., None]
    kd = jnp.where(idx[None, :] < valid, jnp.exp(lg[:, None] * (valid - 1.0 - idx[None, :])), 0.0)[..., None]
    gl = jnp.exp(lg * valid)[:, None, None]
    return dm, qd, kd, gl


def _ret_prompt(z, *, batch, seq, chunk=RET_CHUNK):
    dm, qd, kd, gl = _ret_tables(chunk, chunk)
    cpb = 1024 // DK_RET
    kern = functools.partial(_ret_kernel, chunk=chunk)
    zspec = lambda sec: pl.BlockSpec((seq, DK_RET), lambda b, h: (b, sec * cpb + h))
    tab = lambda shape: pl.BlockSpec((1,) + shape, lambda b, h: (h, 0, 0))
    return pl.pallas_call(
        kern,
        grid=(batch, H_RET),
        in_specs=[zspec(ZB_QR), zspec(ZB_KR), zspec(ZB_VR), zspec(ZB_GR),
                  tab((chunk, chunk)), tab((chunk, 1)), tab((chunk, 1)), tab((1, 1))],
        out_specs=[pl.BlockSpec((seq, DV_RET), lambda b, h: (b, h)),
                   pl.BlockSpec((1, 1, DK_RET, DV_RET), lambda b, h: (b, h, 0, 0))],
        out_shape=[jax.ShapeDtypeStruct((batch * seq, W_RET), BF16),
                   jax.ShapeDtypeStruct((batch, H_RET, DK_RET, DV_RET), F32)],
        scratch_shapes=[pltpu.VMEM((DK_RET, DV_RET), F32)],
        compiler_params=_params(("arbitrary", "arbitrary")),
        name="ret_prompt",
    )(z, z, z, z, dm, qd, kd, gl)


def _ret_sample_kernel(q_ref, k_ref, v_ref, sg_ref, st_ref, dm_ref, qd_ref, kd_ref, gl_ref, a_ref, nst_ref):
    tp = q_ref.shape[1]
    pad = jnp.zeros((PAGE - tp, DK_RET), BF16)
    for h in range(H_RET):
        cols = slice(h * DK_RET, (h + 1) * DK_RET)
        q = q_ref[0, :, cols]
        k = k_ref[0, :, cols]
        v = v_ref[0, :, cols]
        kp = jnp.concatenate([k, pad], axis=0)
        vp = jnp.concatenate([v, pad], axis=0)
        kdp = jnp.concatenate([(k.astype(F32) * kd_ref[h]).astype(BF16), pad], axis=0)
        st = st_ref[0, h]
        sc = lax.dot_general(q, kp, NT_DIMS, preferred_element_type=F32) * dm_ref[h]
        o = (jnp.dot(sc.astype(BF16), vp, preferred_element_type=F32)
             + qd_ref[h] * jnp.dot(q, st.astype(BF16), preferred_element_type=F32))
        nst_ref[0, h] = gl_ref[h] * st + lax.dot_general(kdp, vp, TN_DIMS, preferred_element_type=F32)
        a_ref[0, :, cols] = _head_norm_gate(o, sg_ref[0, :, cols])


def _ret_sample(z3, state, *, valid):
    b, tp, _ = z3.shape
    dm, qd, kd, gl = _ret_tables(tp, valid)
    dm = jnp.pad(dm, ((0, 0), (0, 0), (0, PAGE - tp)))
    zspec = lambda sec: pl.BlockSpec((1, tp, 1024), lambda i: (i, 0, sec))
    full = lambda a: pl.BlockSpec(a.shape, lambda i: (0,) * a.ndim)
    st_spec = pl.BlockSpec((1, H_RET, DK_RET, DV_RET), lambda i: (i, 0, 0, 0))
    return pl.pallas_call(
        _ret_sample_kernel,
        grid=(b,),
        in_specs=[zspec(ZB_QR), zspec(ZB_KR), zspec(ZB_VR), zspec(ZB_GR), st_spec,
                  full(dm), full(qd), full(kd), full(gl)],
        out_specs=[pl.BlockSpec((1, tp, W_RET), lambda i: (i, 0, 0)), st_spec],
        out_shape=[jax.ShapeDtypeStruct((b, tp, W_RET), BF16),
                   jax.ShapeDtypeStruct(state.shape, F32)],
        compiler_params=_params(("arbitrary",)),
        name="ret_sample",
    )(z3, z3, z3, z3, state, dm, qd, kd, gl)


def _fox_bias_kernel(pt_ref, *refs, pps):
    del pt_ref
    lf_refs = refs[:pps]
    o_ref, carry_scr = refs[pps], refs[pps + 1]

    @pl.when(pl.program_id(1) == 0)
    def _():
        carry_scr[...] = jnp.zeros_like(carry_scr)

    lf = jnp.concatenate([r[0] for r in lf_refs], axis=0)
    nl = lf.shape[0]
    jj = lax.broadcasted_iota(jnp.int32, (PAGE, 2 * PAGE), 0)
    ss = lax.broadcasted_iota(jnp.int32, (PAGE, 2 * PAGE), 1)
    same_head = (jj % H_FOX) == (ss % H_FOX)
    w = jnp.where(same_head & ((ss >= PAGE) | (jj // H_FOX > ss // H_FOX)), 1.0, 0.0).astype(BF16)
    parts = jnp.dot(jnp.concatenate(_split3(lf), axis=0), w, preferred_element_type=F32)
    both = parts[0:nl] + parts[nl:2 * nl] + parts[2 * nl:3 * nl]
    within = both[:, :PAGE]
    tot = both[:, PAGE:]
    kdim = max(nl, PAGE)
    ri = lax.broadcasted_iota(jnp.int32, (nl, kdim), 0)
    ci = lax.broadcasted_iota(jnp.int32, (nl, kdim), 1)
    later = jnp.where((ci > ri) & (ci < nl), 1.0, 0.0).astype(BF16)
    t3 = jnp.concatenate(_split3(tot), axis=1)
    if nl < kdim:
        t3 = jnp.concatenate([t3, jnp.zeros((kdim - nl, 3 * PAGE), BF16)], axis=0)
    cross3 = jnp.dot(later, t3, preferred_element_type=F32)
    cross = cross3[:, :PAGE] + cross3[:, PAGE:2 * PAGE] + cross3[:, 2 * PAGE:]
    carry = carry_scr[...]
    r = within + cross + carry
    carry_scr[...] = carry + jnp.sum(tot, axis=0, keepdims=True)
    rows_per_page = PAGE * H_FOX // PAGE
    for pg in range(pps):
        for c in range(rows_per_page):
            o_ref[0, pg, :, c * PAGE:(c + 1) * PAGE] = r[pg * rows_per_page + c:pg * rows_per_page + c + 1, :]


def _fox_bias(cache_lf, page_table, *, pps):
    b, n_pages = page_table.shape
    steps = n_pages // pps
    rpp = cache_lf.shape[1]

    def page_map(r):
        return lambda i, p, pt: (pt[i * n_pages + (steps - 1 - p) * pps + r], 0, 0)

    grid_spec = pltpu.PrefetchScalarGridSpec(
        num_scalar_prefetch=1,
        grid=(b, steps),
        in_specs=[pl.BlockSpec((1, rpp, PAGE), page_map(r)) for r in range(pps)],
        out_specs=pl.BlockSpec((1, pps, 1, rpp * PAGE), lambda i, p, pt: (i, steps - 1 - p, 0, 0)),
        scratch_shapes=[pltpu.VMEM((1, PAGE), F32)],
    )
    return pl.pallas_call(
        functools.partial(_fox_bias_kernel, pps=pps),
        grid_spec=grid_spec,
        out_shape=jax.ShapeDtypeStruct((b, n_pages, 1, rpp * PAGE), F32),
        compiler_params=_params(("arbitrary", "arbitrary")),
        name="fox_bias",
    )(page_table.reshape(-1), *([cache_lf] * pps))


def _heads_to_rows(x_row):
    return jnp.concatenate([x_row[:, h * HD_FOX:(h + 1) * HD_FOX] for h in range(H_FOX)], axis=0)


def _fox_sample_kernel(pt_ref, *refs, n_new, scale, pps):
    del pt_ref
    k_refs = refs[0:pps]
    v_refs = refs[pps:2 * pps]
    r_ref, q_ref, kn_ref, vn_ref, sg_ref, lfn_ref, o_ref = refs[2 * pps:2 * pps + 7]
    q_scr, m_scr, l_scr, acc_scr, g_scr = refs[2 * pps + 7:]
    p_id = pl.program_id(1)
    n_rows = n_new * H_FOX
    n_keys = PAGE * H_FOX

    @pl.when(p_id == 0)
    def _():
        q = q_ref[0].astype(F32)
        q_scr[...] = jnp.concatenate([_heads_to_rows(q[t:t + 1, :]) for t in range(n_new)], axis=0).astype(BF16)
        m_scr[...] = jnp.full_like(m_scr, -jnp.inf)
        l_scr[...] = jnp.zeros_like(l_scr)
        acc_scr[...] = jnp.zeros_like(acc_scr)
        lfn = lfn_ref[0]
        r8 = lax.broadcasted_iota(jnp.int32, (H_FOX, H_FOX), 0)
        c8 = lax.broadcasted_iota(jnp.int32, (H_FOX, H_FOX), 1)
        g = jnp.zeros((1, H_FOX), F32)
        cols = []
        for t in range(n_new):
            g = g + lfn[t:t + 1, :]
            cols.append(jnp.sum(jnp.where(r8 == c8, jnp.broadcast_to(g, (H_FOX, H_FOX)), 0.0), axis=1, keepdims=True))
        g_scr[...] = jnp.concatenate(cols, axis=0)

    qm = q_scr[...]
    gcol = g_scr[...]
    row = lax.broadcasted_iota(jnp.int32, (n_rows, n_keys), 0)
    lane = lax.broadcasted_iota(jnp.int32, (n_rows, n_keys), 1)
    own_head = (lane % H_FOX) == (row % H_FOX)

    logits = []
    for r in range(pps):
        kp = k_refs[r][0].reshape(n_keys, HD_FOX).astype(BF16)
        s = lax.dot_general(qm, kp, NT_DIMS, preferred_element_type=F32)
        logits.append(jnp.where(own_head, s * scale + gcol + r_ref[0, r], -jnp.inf))
    m_old = m_scr[...]
    m_new = m_old
    for lg in logits:
        m_new = jnp.maximum(m_new, jnp.max(lg, axis=1, keepdims=True))
    alpha = jnp.exp(m_old - m_new)
    l = alpha * l_scr[...]
    acc = alpha * acc_scr[...]
    for r, lg in enumerate(logits):
        p = jnp.exp(lg - m_new)
        l = l + jnp.sum(p, axis=1, keepdims=True)
        vp = v_refs[r][0].reshape(n_keys, HD_FOX).astype(BF16)
        acc = acc + jnp.dot(p.astype(BF16), vp, preferred_element_type=F32)
    m_scr[...] = m_new
    l_scr[...] = l
    acc_scr[...] = acc

    @pl.when(p_id == pl.num_programs(1) - 1)
    def _():
        qf = q_scr[...].astype(F32)
        kn = kn_ref[0].astype(F32)
        vn = vn_ref[0].astype(F32)
        g_col = g_scr[...]
        tok = lax.broadcasted_iota(jnp.int32, (n_rows, 1), 0) // H_FOX
        tile = lambda x: jnp.concatenate([x] * n_new, axis=0)
        lgs = []
        for t in range(n_new):
            sn = jnp.sum(qf * tile(_heads_to_rows(kn[t:t + 1, :])), axis=1, keepdims=True) * scale
            g_t = tile(g_col[t * H_FOX:(t + 1) * H_FOX, :])
            lgs.append(jnp.where(tok >= t, sn + g_col - g_t, -jnp.inf))
        m0 = m_scr[...]
        m1 = m0
        for lg in lgs:
            m1 = jnp.maximum(m1, lg)
        a1 = jnp.exp(m0 - m1)
        l1 = a1 * l_scr[...]
        acc1 = a1 * acc_scr[...]
        for t, lg in enumerate(lgs):
            pt = jnp.exp(lg - m1)
            l1 = l1 + pt
            acc1 = acc1 + pt * tile(_heads_to_rows(vn[t:t + 1, :]))
        o = acc1 / l1
        outs = [jnp.concatenate([o[t * H_FOX + h:t * H_FOX + h + 1, :] for h in range(H_FOX)], axis=1)
                for t in range(n_new)]
        outs.append(jnp.zeros((o_ref.shape[1] - n_new, W_FOX), F32))
        o_ref[0] = (jnp.concatenate(outs, axis=0) * sg_ref[0].astype(F32)).astype(BF16)


def _fox_sample(z3, lf_new, cache_k, cache_v, bias_rows, page_table, *, n_new, pps):
    b, tp, _ = z3.shape
    n_pages = page_table.shape[1]
    steps = n_pages // pps
    n_rows = n_new * H_FOX

    def page_map(r):
        return lambda i, p, pt: (pt[i * n_pages + (steps - 1 - p) * pps + r], 0, 0, 0)

    kv_specs = [pl.BlockSpec((1, PAGE, H_FOX, HD_FOX), page_map(r)) for r in range(pps)]
    zspec = lambda sec: pl.BlockSpec((1, tp, 1024), lambda i, p, pt: (i, 0, sec))
    kern = functools.partial(_fox_sample_kernel, n_new=n_new, scale=HD_FOX ** -0.5, pps=pps)
    grid_spec = pltpu.PrefetchScalarGridSpec(
        num_scalar_prefetch=1,
        grid=(b, steps),
        in_specs=kv_specs + kv_specs + [
            pl.BlockSpec((1, pps, 1, PAGE * H_FOX), lambda i, p, pt: (i, steps - 1 - p, 0, 0)),
            zspec(ZB_QF), zspec(ZB_KF), zspec(ZB_VF), zspec(ZB_GF),
            pl.BlockSpec((1, tp, H_FOX), lambda i, p, pt: (i, 0, 0)),
        ],
        out_specs=pl.BlockSpec((1, tp, W_FOX), lambda i, p, pt: (i, 0, 0)),
        scratch_shapes=[
            pltpu.VMEM((n_rows, HD_FOX), BF16),
            pltpu.VMEM((n_rows, 1), F32),
            pltpu.VMEM((n_rows, 1), F32),
            pltpu.VMEM((n_rows, HD_FOX), F32),
            pltpu.VMEM((n_rows, 1), F32),
        ],
    )
    return pl.pallas_call(
        kern,
        grid_spec=grid_spec,
        out_shape=jax.ShapeDtypeStruct((b, tp, W_FOX), BF16),
        compiler_params=_params(("arbitrary", "arbitrary")),
        name="fox_sample",
    )(page_table.reshape(-1), *([cache_k] * pps), *([cache_v] * pps), bias_rows, z3, z3, z3, z3, lf_new)


def _merge_kernel(ar_ref, af_ref, mr_ref, mf_ref, x_ref, gate_ref, wb_ref, wo_ref, gf_ref, y_ref, *, rc):
    tm = x_ref.shape[0]
    for c0 in range(0, tm, rc):
        rows = slice(c0, c0 + rc)
        p_r = jnp.dot(ar_ref[rows, :], wb_ref[0:W_RET, :], preferred_element_type=F32)
        p_f = jnp.dot(af_ref[rows, :], wb_ref[W_RET:W_RET + W_FOX, :], preferred_element_type=F32)
        merged = mr_ref[rows, :].astype(F32) * p_r + mf_ref[rows, :].astype(F32) * p_f
        gate = gate_ref[0] if gate_ref.shape[1] == 1 else gate_ref[0, rows, :]
        out = x_ref[rows, :] + gate * jnp.dot(merged.astype(BF16), wo_ref[...], preferred_element_type=F32)
        y = out * lax.rsqrt(jnp.mean(out * out, axis=-1, keepdims=True) + EPS)
        y_ref[rows, :] = y * gf_ref[...]


def _merge(a_r, a_f, z, x2d, gate, w_branch, w_out, g_final, *, tm):
    rows, d = x2d.shape
    nrb = rows // tm
    groups, r, _ = gate.shape
    bpg = nrb // groups
    return pl.pallas_call(
        functools.partial(_merge_kernel, rc=min(MERGE_ROW_CHUNK, tm)),
        grid=(nrb,),
        in_specs=[
            pl.BlockSpec((tm, W_RET), lambda i: (i, 0)),
            pl.BlockSpec((tm, W_FOX), lambda i: (i, 0)),
            pl.BlockSpec((tm, d), lambda i: (i, 4)),
            pl.BlockSpec((tm, d), lambda i: (i, 5)),
            pl.BlockSpec((tm, d), lambda i: (i, 0)),
            pl.BlockSpec((1, r, d), lambda i: (i // bpg, 0, 0)),
            pl.BlockSpec(w_branch.shape, lambda i: (0, 0)),
            pl.BlockSpec(w_out.shape, lambda i: (0, 0)),
            pl.BlockSpec((1, d), lambda i: (0, 0)),
        ],
        out_specs=pl.BlockSpec((tm, d), lambda i: (i, 0)),
        out_shape=jax.ShapeDtypeStruct((rows, d), F32),
        compiler_params=_params(("arbitrary",)),
        name="merge",
    )(a_r, a_f, z, z, x2d, gate, w_branch, w_out, g_final.reshape(1, d))


def _rope_tables(pos):
    inv = 1.0 / (ROPE_BASE ** (jnp.arange(0, DK_RET, 2, dtype=F32) / DK_RET))
    ang = pos.astype(F32)[:, None] * inv[None, :]
    return jnp.cos(ang), jnp.sin(ang)


def kernel(x_prompt, x_sample, c_prompt, c_sample, cache_k, cache_v, cache_logf, state_ret, page_table,
           w_in, b_fgt, g_norm, w_ada, b_ada, w_branch, w_out, g_final):
    depth = w_in.shape[0]
    assert depth == 1, "the layer loop is written for the single-layer configuration"
    batch, seq, d = x_prompt.shape
    dec_batch, dec_seq, _ = x_sample.shape
    n_pages = page_table.shape[1]
    past_len = n_pages * PAGE
    l = 0

    w_l = w_in[l]
    slabs = lambda w: w.astype(BF16).reshape(d, w.shape[1] // PROJ_WN, PROJ_WN).transpose(1, 0, 2)
    w_lo = slabs(w_l[:, :F_COL0])
    w_hi = slabs(w_l[:, F_COL0 + H_FOX:])
    w_f = jnp.pad(w_l[:, F_COL0:F_COL0 + H_FOX], ((0, 0), (0, 128 - H_FOX))).astype(BF16)
    wb = w_branch[l].astype(BF16)
    wo = w_out[l].astype(BF16)

    mod = _mod(jnp.concatenate([c_prompt, c_sample], axis=0), w_ada[l], b_ada[l])
    shift, scale, gate = jnp.split(mod, 3, axis=-1)

    rows_p = batch * seq
    xp2 = x_prompt.reshape(rows_p, d)
    cos_p, sin_p = _rope_tables(jnp.arange(seq))
    h_p, lf_p = _norm(xp2, scale[:batch, None, :], shift[:batch, None, :], g_norm[l], w_f, b_fgt[l],
                      tm=min(512, seq))
    z_p, kf_p, vf_p = _proj(h_p, w_lo, w_hi, cos_p, sin_p, tm=min(2048, seq))
    t_fox = min(1024, seq)
    f_t = _cumsum_t(lf_p.reshape(batch, seq, H_FOX).transpose(0, 2, 1), chunk=min(256, seq))
    a_f_p = _fox_prompt(z_p, f_t.reshape(batch * H_FOX, seq // t_fox, 1, t_fox),
                        batch=batch, seq=seq, t=t_fox)
    a_r_p, st_p = _ret_prompt(z_p, batch=batch, seq=seq)
    y_p = _merge(a_r_p, a_f_p, z_p, xp2, gate[:batch, None, :], wb, wo, g_final, tm=min(256, rows_p))

    rows_s = dec_batch * T_PAD
    xs2 = jnp.pad(x_sample, ((0, 0), (0, T_PAD - dec_seq), (0, 0))).reshape(rows_s, d)
    pos_s = past_len + (jnp.arange(rows_s) % T_PAD)
    cos_s, sin_s = _rope_tables(pos_s)
    tm_s = min(256, rows_s)
    rep = lambda a: jnp.repeat(a[batch:], T_PAD, axis=0).reshape(rows_s // tm_s, tm_s, d)
    h_s, lf_s = _norm(xs2, rep(scale), rep(shift), g_norm[l], w_f, b_fgt[l], tm=tm_s)
    z_s, kf_s, vf_s = _proj(h_s, w_lo, w_hi, cos_s, sin_s, tm=tm_s)
    z_s3 = z_s.reshape(dec_batch, T_PAD, Z_COLS)
    a_r_s, st_s = _ret_sample(z_s3, state_ret[l].astype(F32), valid=dec_seq)
    n_pool = cache_k.shape[1]
    pps = min(PAGES_PER_STEP, n_pages)
    bias = _fox_bias(cache_logf[l].reshape(n_pool, H_FOX, PAGE), page_table,
                     pps=min(BIAS_PAGES_PER_STEP, n_pages))
    a_f_s = _fox_sample(z_s3, lf_s.reshape(dec_batch, T_PAD, H_FOX), cache_k[l], cache_v[l],
                        bias, page_table, n_new=dec_seq, pps=pps)
    y_s = _merge(a_r_s.reshape(rows_s, W_RET), a_f_s.reshape(rows_s, W_FOX), z_s, xs2, rep(gate),
                 wb, wo, g_final, tm=tm_s)

    take = lambda a, tail: a.reshape((dec_batch, T_PAD) + tail)[:, :dec_seq]
    return (y_p.reshape(batch, seq, d),
            take(y_s, (d,)),
            kf_p.reshape(1, batch, seq, H_FOX, HD_FOX),
            vf_p.reshape(1, batch, seq, H_FOX, HD_FOX),
            lf_p.reshape(1, batch, seq, H_FOX),
            st_p[None],
            take(kf_s, (H_FOX, HD_FOX))[None],
            take(vf_s, (H_FOX, HD_FOX))[None],
            take(lf_s, (H_FOX,))[None],
            st_s[None])
```
